```python
import jax, jax.numpy as jnp
from jax import lax
import numpy as np

D_MODEL = 4096
BATCH = 2
SEQ = 4096
DEPTH = 2

CHUNK = 64
Q_BLOCK = 128
SB_HEADS = 16
SB_HEAD_DIM = 128
MLA_HEADS = 16
MLA_NOPE_DIM = 128
MLA_ROPE_DIM = 64
MLA_V_DIM = 128
MLA_Q_RANK = 1024
MLA_KV_RANK = 512
D_FF = 4 * D_MODEL
ROPE_THETA = 10000.0
EPS = 1e-6
SB_WIDTH = SB_HEADS * SB_HEAD_DIM
MLA_WIDTH = MLA_HEADS * MLA_V_DIM
IN_WIDTH = 3 * SB_WIDTH + MLA_Q_RANK + MLA_KV_RANK + MLA_ROPE_DIM + 2 * D_MODEL

kernel_name = "hybrid_stickbreak_mla_block"


def _in_split_points():
    sizes = [SB_WIDTH, SB_WIDTH, SB_WIDTH, MLA_Q_RANK, MLA_KV_RANK + MLA_ROPE_DIM, D_MODEL, D_MODEL]
    pts, acc = [], 0
    for s in sizes[:-1]:
        acc += s
        pts.append(acc)
    return pts


def rms_norm(x, g):
    xf = x.astype(jnp.float32)
    y = xf * lax.rsqrt(jnp.mean(xf * xf, axis=-1, keepdims=True) + EPS)
    return (y * g.astype(jnp.float32)).astype(x.dtype)


def modulate(h, shift, scale):
    return h * (1.0 + scale[:, None, :]) + shift[:, None, :]


def rope_tables(positions, dtype):
    half = MLA_ROPE_DIM // 2
    inv_freq = ROPE_THETA ** (-jnp.arange(half, dtype=jnp.float32) / half)
    ang = positions.astype(jnp.float32)[..., None] * inv_freq
    return jnp.cos(ang).astype(dtype), jnp.sin(ang).astype(dtype)


def apply_rope(x, cos, sin):
    x1, x2 = jnp.split(x, 2, axis=-1)
    return jnp.concatenate([x1 * cos - x2 * sin, x2 * cos + x1 * sin], axis=-1)


def stick_breaking_attention(q, k, v):
    B, S, H, Dh = q.shape
    nb = S // Q_BLOCK
    scale = Dh ** -0.5
    q_blocks = q.reshape(B, nb, Q_BLOCK, H, Dh).swapaxes(0, 1)
    key_pos = jnp.arange(S)

    def block(args):
        q_blk, i = args
        z = jnp.einsum('bqhd,bkhd->bhqk', q_blk, k, preferred_element_type=jnp.float32) * scale
        q_pos = i * Q_BLOCK + jnp.arange(Q_BLOCK)
        strict = key_pos[None, :] < q_pos[:, None]
        log_fail = jnp.where(strict, jax.nn.log_sigmoid(-z), 0.0)
        after = lax.cumsum(log_fail, axis=3, reverse=True) - log_fail
        w = jnp.where(strict, jnp.exp(jax.nn.log_sigmoid(z) + after), 0.0)
        return jnp.einsum('bhqk,bkhd->bqhd', w, v.astype(jnp.float32))

    out = lax.map(block, (q_blocks, jnp.arange(nb)))
    return out.swapaxes(0, 1).reshape(B, S, H * Dh).astype(q.dtype)


def latent_attention(q_nope, q_rope, k_nope, k_rope, v):
    B, S, H, Dn = q_nope.shape
    Dv = v.shape[-1]
    nb = S // Q_BLOCK
    scale = (Dn + q_rope.shape[-1]) ** -0.5
    qn_b = q_nope.reshape(B, nb, Q_BLOCK, H, Dn).swapaxes(0, 1)
    qr_b = q_rope.reshape(B, nb, Q_BLOCK, H, -1).swapaxes(0, 1)
    key_chunk = jnp.arange(S) // CHUNK

    def block(args):
        qn, qr, i = args
        s = (jnp.einsum('bqhd,bkhd->bhqk', qn, k_nope, preferred_element_type=jnp.float32)
             + jnp.einsum('bqhr,bkr->bhqk', qr, k_rope, preferred_element_type=jnp.float32)) * scale
        q_chunk = (i * Q_BLOCK + jnp.arange(Q_BLOCK)) // CHUNK
        mask = key_chunk[None, :] <= q_chunk[:, None]
        p = jax.nn.softmax(jnp.where(mask, s, -jnp.inf), axis=-1)
        return jnp.einsum('bhqk,bkhd->bqhd', p, v.astype(jnp.float32))

    out = lax.map(block, (qn_b, qr_b, jnp.arange(nb)))
    return out.swapaxes(0, 1).reshape(B, S, H * Dv).astype(v.dtype)


def setup_inputs(seed: int = 0) -> dict:
    key = jax.random.key(seed)
    ks = jax.random.split(key, 24)
    L = DEPTH

    def nrm(k, shape, scale):
        return jax.random.normal(k, shape, jnp.float32) * scale

    def gain(k, n):
        return 1.0 + 0.1 * nrm(k, (L, n), 1.0)

    x = nrm(ks[0], (BATCH, SEQ, D_MODEL), 1.0)
    c = nrm(ks[1], (BATCH, D_MODEL), 1.0)
    offset = jax.random.randint(ks[2], (BATCH, 1), 0, 4096)
    positions = (offset + jnp.arange(SEQ)[None, :]).astype(jnp.int32)
    return {
        "x": x,
        "c": c,
        "positions": positions,
        "w_ada": nrm(ks[3], (L, D_MODEL, 6 * D_MODEL), 0.5 * D_MODEL ** -0.5),
        "b_ada": nrm(ks[4], (L, 6 * D_MODEL), 0.02),
        "g_pre_mix": gain(ks[5], D_MODEL),
        "g_post_mix": gain(ks[6], D_MODEL),
        "g_pre_mlp": gain(ks[7], D_MODEL),
        "g_post_mlp": gain(ks[8], D_MODEL),
        "w_in": nrm(ks[9], (L, D_MODEL, IN_WIDTH), D_MODEL ** -0.5),
        "g_q_lora": gain(ks[10], MLA_Q_RANK),
        "w_uq": nrm(ks[11], (L, MLA_Q_RANK, MLA_HEADS * (MLA_NOPE_DIM + MLA_ROPE_DIM)), MLA_Q_RANK ** -0.5),
        "g_kv_lora": gain(ks[12], MLA_KV_RANK),
        "w_ukv": nrm(ks[13], (L, MLA_KV_RANK, MLA_HEADS * (MLA_NOPE_DIM + MLA_V_DIM)), MLA_KV_RANK ** -0.5),
        "w_o_sb": nrm(ks[14], (L, SB_WIDTH, D_MODEL), SB_WIDTH ** -0.5),
        "w_o_mla": nrm(ks[15], (L, MLA_WIDTH, D_MODEL), MLA_WIDTH ** -0.5),
        "w_out": nrm(ks[16], (L, D_MODEL, D_MODEL), D_MODEL ** -0.5),
        "w_up": nrm(ks[17], (L, D_MODEL, D_FF), D_MODEL ** -0.5),
        "w_down": nrm(ks[18], (L, D_FF, D_MODEL), D_FF ** -0.5),
    }


def reference(x, c, positions, w_ada, b_ada, g_pre_mix, g_post_mix, g_pre_mlp, g_post_mlp, w_in,
              g_q_lora, w_uq, g_kv_lora, w_ukv, w_o_sb, w_o_mla, w_out, w_up, w_down):
    B, S, _ = x.shape
    cos, sin = rope_tables(positions, x.dtype)
    split_pts = _in_split_points()
    c_act = jax.nn.silu(c)
    for l in range(DEPTH):
        ada = c_act @ w_ada[l] + b_ada[l]
        sh1, sc1, gt1, sh2, sc2, gt2 = jnp.split(ada, 6, axis=-1)

        h = modulate(rms_norm(x, g_pre_mix[l]), sh1, sc1)
        proj = h @ w_in[l]
        q_sb, k_sb, v_sb, q_down, kv_down, gl_sb, gl_mla = jnp.split(proj, split_pts, axis=-1)

        o_sb = stick_breaking_attention(
            q_sb.reshape(B, S, SB_HEADS, SB_HEAD_DIM),
            k_sb.reshape(B, S, SB_HEADS, SB_HEAD_DIM),
            v_sb.reshape(B, S, SB_HEADS, SB_HEAD_DIM))
        br_sb = o_sb @ w_o_sb[l]

        c_q = rms_norm(q_down, g_q_lora[l])
        q = (c_q @ w_uq[l]).reshape(B, S, MLA_HEADS, MLA_NOPE_DIM + MLA_ROPE_DIM)
        q_nope, q_rope = q[..., :MLA_NOPE_DIM], q[..., MLA_NOPE_DIM:]
        q_rope = apply_rope(q_rope, cos[:, :, None, :], sin[:, :, None, :])
        c_kv = rms_norm(kv_down[..., :MLA_KV_RANK], g_kv_lora[l])
        k_rope = apply_rope(kv_down[..., MLA_KV_RANK:], cos, sin)
        kv = (c_kv @ w_ukv[l]).reshape(B, S, MLA_HEADS, MLA_NOPE_DIM + MLA_V_DIM)
        k_nope, v_mla = kv[..., :MLA_NOPE_DIM], kv[..., MLA_NOPE_DIM:]
        o_mla = latent_attention(q_nope, q_rope, k_nope, k_rope, v_mla)
        br_mla = o_mla @ w_o_mla[l]

        merged = jax.nn.sigmoid(gl_sb) * br_sb + jax.nn.sigmoid(gl_mla) * br_mla
        y = merged @ w_out[l]
        x = x + gt1[:, None, :] * rms_norm(y, g_post_mix[l])

        h = modulate(rms_norm(x, g_pre_mlp[l]), sh2, sc2)
        y = jnp.square(jax.nn.relu(h @ w_up[l])) @ w_down[l]
        x = x + gt2[:, None, :] * rms_norm(y, g_post_mlp[l])
    return x
```

```python
import functools

import jax
import jax.numpy as jnp
from jax import lax
from jax.experimental import pallas as pl
from jax.experimental.pallas import tpu as pltpu

F32 = jnp.float32
BF16 = jnp.bfloat16

HEADS = 16
HEAD_DIM = 128
ATT_WIDTH = HEADS * HEAD_DIM
ROPE_DIM = 64
Q_RANK = 1024
KV_RANK = 512
CHUNK = 64
ROPE_THETA = 10000.0
EPS = 1e-6
NEG_BIG = -1e30

COL_Q, COL_K, COL_V = 0, ATT_WIDTH, 2 * ATT_WIDTH
COL_QD = 3 * ATT_WIDTH
COL_KVD = COL_QD + Q_RANK
COL_GATE = COL_KVD + 1024

V7X_VMEM_LIMIT_BYTES = 56 * 1024 * 1024
LANE = 128


def _params(sem):
    return pltpu.CompilerParams(dimension_semantics=sem, vmem_limit_bytes=V7X_VMEM_LIMIT_BYTES)


def _tile(n, pref):
    if n <= pref:
        return n
    t = (pref // LANE) * LANE
    while t >= LANE:
        if n % t == 0:
            return t
        t -= LANE
    raise ValueError(f"no tile for {n}")


def _ada_kernel(c_ref, w_ref, b_ref, o_ref):
    c = c_ref[...]
    act = c * (1.0 / (1.0 + jnp.exp(-c)))
    o_ref[...] = jnp.dot(act.astype(BF16), w_ref[...].astype(BF16),
                         preferred_element_type=F32) + b_ref[...]


def _ada(c_pad, w_ada, b_ada):
    L, D, N = w_ada.shape
    R = c_pad.shape[0]
    tn = _tile(N, 512)
    return pl.pallas_call(
        _ada_kernel,
        grid=(L, N // tn),
        in_specs=[pl.BlockSpec((R, D), lambda l, j: (0, 0)),
                  pl.BlockSpec((None, D, tn), lambda l, j: (l, 0, j)),
                  pl.BlockSpec((None, 1, tn), lambda l, j: (l, 0, j))],
        out_specs=pl.BlockSpec((None, R, tn), lambda l, j: (l, 0, j)),
        out_shape=jax.ShapeDtypeStruct((L, R, N), F32),
        compiler_params=_params(("parallel", "parallel")),
        name="ada_proj",
    )(c_pad, w_ada, b_ada.reshape(L, 1, N))


def _rope_table_kernel(pos_ref, inv_ref, sgn_ref, cos_ref, sin_ref):
    ang = pos_ref[...].astype(F32) * inv_ref[...]
    cos_ref[...] = jnp.cos(ang)
    sin_ref[...] = jnp.sin(ang) * sgn_ref[...]


def _rope_tables(positions):
    T = positions.size
    half = ROPE_DIM // 2
    inv_freq = ROPE_THETA ** (-jnp.arange(half, dtype=F32) / half)
    zeros = jnp.zeros((LANE - ROPE_DIM,), F32)
    inv = jnp.concatenate([inv_freq, inv_freq, zeros]).reshape(1, LANE)
    sgn = jnp.concatenate([-jnp.ones((half,), F32), jnp.ones((half,), F32), zeros]).reshape(1, LANE)
    tm = _tile(T, 1024)
    tab = jax.ShapeDtypeStruct((T, LANE), F32)
    return pl.pallas_call(
        _rope_table_kernel,
        grid=(T // tm,),
        in_specs=[pl.BlockSpec((tm, 1), lambda i: (i, 0)),
                  pl.BlockSpec((1, LANE), lambda i: (0, 0)),
                  pl.BlockSpec((1, LANE), lambda i: (0, 0))],
        out_specs=[pl.BlockSpec((tm, LANE), lambda i: (i, 0))] * 2,
        out_shape=[tab, tab],
        compiler_params=_params(("parallel",)),
        name="rope_tables",
    )(positions.reshape(T, 1), inv, sgn)


def _rms(x, g):
    return x * lax.rsqrt(jnp.mean(x * x, axis=-1, keepdims=True) + EPS) * g


def _ada_spec(D, j, tiles_per_batch):
    return pl.BlockSpec((None, None, 1, D), lambda i: (i // tiles_per_batch, j, 0, 0))


def _prenorm_kernel(x_ref, g_ref, sh_ref, sc_ref, h_ref):
    h = _rms(x_ref[...], g_ref[...])
    h_ref[...] = (h * (1.0 + sc_ref[...]) + sh_ref[...]).astype(BF16)


def _prenorm(x2d, g, ada_l, j_shift, j_scale, S):
    T, D = x2d.shape
    tm = _tile(S, 512)
    tpb = S // tm
    row = pl.BlockSpec((tm, D), lambda i: (i, 0))
    vec = pl.BlockSpec((1, D), lambda i: (0, 0))
    return pl.pallas_call(
        _prenorm_kernel,
        grid=(T // tm,),
        in_specs=[row, vec, _ada_spec(D, j_shift, tpb), _ada_spec(D, j_scale, tpb)],
        out_specs=row,
        out_shape=jax.ShapeDtypeStruct((T, D), BF16),
        compiler_params=_params(("parallel",)),
        name="prenorm",
    )(x2d, g.reshape(1, D), ada_l, ada_l)


def _post_kernel(y_ref, x_ref, gpost_ref, gate_ref, *rest, with_next):
    y = y_ref[...].astype(F32)
    xn = x_ref[...] + gate_ref[...] * _rms(y, gpost_ref[...])
    if with_next:
        gpre_ref, sh_ref, sc_ref, xo_ref, h_ref = rest
        h = _rms(xn, gpre_ref[...])
        h_ref[...] = (h * (1.0 + sc_ref[...]) + sh_ref[...]).astype(BF16)
    else:
        (xo_ref,) = rest
    xo_ref[...] = xn


def _post(y, x2d, g_post, ada_l, j_gate, S, nxt=None):
    T, D = x2d.shape
    tm = _tile(S, 256)
    tpb = S // tm
    row = pl.BlockSpec((tm, D), lambda i: (i, 0))
    vec = pl.BlockSpec((1, D), lambda i: (0, 0))
    in_specs = [row, row, vec, _ada_spec(D, j_gate, tpb)]
    args = [y, x2d, g_post.reshape(1, D), ada_l]
    out_specs = [row]
    out_shape = [jax.ShapeDtypeStruct((T, D), F32)]
    if nxt is not None:
        g_pre, ada_n, j_shift, j_scale = nxt
        in_specs += [vec, _ada_spec(D, j_shift, tpb), _ada_spec(D, j_scale, tpb)]
        args += [g_pre.reshape(1, D), ada_n, ada_n]
        out_specs.append(row)
        out_shape.append(jax.ShapeDtypeStruct((T, D), BF16))
    outs = pl.pallas_call(
        functools.partial(_post_kernel, with_next=nxt is not None),
        grid=(T // tm,),
        in_specs=in_specs,
        out_specs=out_specs,
        out_shape=out_shape,
        compiler_params=_params(("parallel",)),
        name="post_norm_residual",
    )(*args)
    return outs if nxt is not None else (outs[0], None)


def _mm_kernel(x_ref, w_ref, o_ref):
    o_ref[...] = jnp.dot(x_ref[...], w_ref[...], preferred_element_type=F32).astype(o_ref.dtype)


def _matmul(x, w, out_dtype, name):
    M, K = x.shape
    N = w.shape[1]
    tm, tn = _tile(M, 1024), _tile(N, 1024)
    return pl.pallas_call(
        _mm_kernel,
        grid=(M // tm, N // tn),
        in_specs=[pl.BlockSpec((tm, K), lambda i, j: (i, 0)),
                  pl.BlockSpec((K, tn), lambda i, j: (0, j))],
        out_specs=pl.BlockSpec((tm, tn), lambda i, j: (i, j)),
        out_shape=jax.ShapeDtypeStruct((M, N), out_dtype),
        compiler_params=_params(("parallel", "arbitrary")),
        name=name,
    )(x, w)


def _sb_kernel(q_ref, k_ref, v_ref, o_ref, acc_ref, carry_ref, *, blk, scale):
    qi = pl.program_id(2)
    q = q_ref[...]
    row = lax.broadcasted_iota(jnp.int32, (blk, blk), 0)
    col = lax.broadcasted_iota(jnp.int32, (blk, blk), 1)
    suffix = jnp.where(row >= col, 1.0, 0.0).astype(BF16)
    strict = col < row

    def scores(off):
        k = k_ref[pl.ds(off, blk), :]
        z = lax.dot_general(q, k, (((1,), (1,)), ((), ())), preferred_element_type=F32) * scale
        log_fail = jnp.minimum(z, 0.0) - jnp.log(1.0 + jnp.exp(-jnp.abs(z))) - z
        return z, log_fail

    def accumulate(off, w):
        v = v_ref[pl.ds(off, blk), :]
        return jnp.dot(w.astype(BF16), v, preferred_element_type=F32)

    off = pl.multiple_of(qi * blk, blk)
    z, lf = scores(off)
    lf = jnp.where(strict, lf, 0.0)
    cs = jnp.dot(lf.astype(BF16), suffix, preferred_element_type=F32)
    w = jnp.where(strict, jnp.exp(z + cs), 0.0)
    acc_ref[...] = accumulate(off, w)
    carry_ref[...] = jnp.sum(lf, axis=1, keepdims=True)

    def body(n, _):
        off = pl.multiple_of((qi - 1 - n) * blk, blk)
        z, lf = scores(off)
        cs = jnp.dot(lf.astype(BF16), suffix, preferred_element_type=F32)
        w = jnp.exp(z + cs + carry_ref[...])
        acc_ref[...] += accumulate(off, w)
        carry_ref[...] += jnp.sum(lf, axis=1, keepdims=True)
        return 0

    lax.fori_loop(0, qi, body, 0)
    o_ref[...] = acc_ref[...].astype(o_ref.dtype)


def _sb_attention(proj3d):
    B, S, _ = proj3d.shape
    blk = _tile(S, 256)
    hb = lambda c: c // HEAD_DIM
    q_spec = pl.BlockSpec((None, blk, HEAD_DIM), lambda b, h, i: (b, i, hb(COL_Q) + h))
    k_spec = pl.BlockSpec((None, S, HEAD_DIM), lambda b, h, i: (b, 0, hb(COL_K) + h))
    v_spec = pl.BlockSpec((None, S, HEAD_DIM), lambda b, h, i: (b, 0, hb(COL_V) + h))
    return pl.pallas_call(
        functools.partial(_sb_kernel, blk=blk, scale=HEAD_DIM ** -0.5),
        grid=(B, HEADS, S // blk),
        in_specs=[q_spec, k_spec, v_spec],
        out_specs=pl.BlockSpec((None, blk, HEAD_DIM), lambda b, h, i: (b, i, h)),
        out_shape=jax.ShapeDtypeStruct((B, S, ATT_WIDTH), BF16),
        scratch_shapes=[pltpu.VMEM((blk, HEAD_DIM), F32), pltpu.VMEM((blk, 1), F32)],
        compiler_params=_params(("parallel", "parallel", "arbitrary")),
        name="stick_breaking_attention",
    )(proj3d, proj3d, proj3d)


def _rope_rotate(xs, cos, sin_signed):
    half = ROPE_DIM // 2
    swapped = pltpu.roll(xs, half, 1) + pltpu.roll(xs, LANE - half, 1)
    return xs * cos + swapped * sin_signed


def _mla_prep_kernel(qd_ref, kvd_ref, gq_ref, gkv_ref, wq_ref, wkv_ref, cos_ref, sin_ref,
                     qn_ref, qr_ref, kn_ref, v_ref, kr_ref, *, scale):
    cos = cos_ref[...]
    sin = sin_ref[...]
    c_q = _rms(qd_ref[...].astype(F32), gq_ref[...]).astype(BF16)
    yq = jnp.dot(c_q, wq_ref[...], preferred_element_type=F32)
    qn_ref[...] = (yq[:, :ATT_WIDTH] * scale).astype(BF16)
    for h in range(HEADS):
        lo = ATT_WIDTH + h * LANE
        qr_ref[:, h * LANE:(h + 1) * LANE] = (
            _rope_rotate(yq[:, lo:lo + LANE], cos, sin) * scale).astype(BF16)
    kvd = kvd_ref[...].astype(F32)
    c_kv = _rms(kvd[:, :KV_RANK], gkv_ref[...]).astype(BF16)
    ykv = jnp.dot(c_kv, wkv_ref[...], preferred_element_type=F32)
    kn_ref[...] = ykv[:, :ATT_WIDTH].astype(BF16)
    v_ref[...] = ykv[:, ATT_WIDTH:].astype(BF16)
    kr_ref[...] = _rope_rotate(kvd[:, KV_RANK:KV_RANK + LANE], cos, sin).astype(BF16)


def _mla_prep(proj, g_q, g_kv, wq, wkv, cos_t, sin_t):
    T = proj.shape[0]
    tm = _tile(T, 256)
    row = lambda w, c: pl.BlockSpec((tm, w), lambda i: (i, c))
    full = lambda a: pl.BlockSpec(a.shape, lambda i: (0, 0))
    wide = jax.ShapeDtypeStruct((T, ATT_WIDTH), BF16)
    return pl.pallas_call(
        functools.partial(_mla_prep_kernel, scale=(HEAD_DIM + ROPE_DIM) ** -0.5),
        grid=(T // tm,),
        in_specs=[row(1024, COL_QD // 1024), row(1024, COL_KVD // 1024),
                  pl.BlockSpec((1, Q_RANK), lambda i: (0, 0)),
                  pl.BlockSpec((1, KV_RANK), lambda i: (0, 0)),
                  full(wq), full(wkv), row(LANE, 0), row(LANE, 0)],
        out_specs=[row(ATT_WIDTH, 0)] * 4 + [row(LANE, 0)],
        out_shape=[wide, wide, wide, wide, jax.ShapeDtypeStruct((T, LANE), BF16)],
        compiler_params=_params(("parallel",)),
        name="latent_projections",
    )(proj, proj, g_q.reshape(1, Q_RANK), g_kv.reshape(1, KV_RANK), wq, wkv, cos_t, sin_t)


def _mla_kernel(qn_ref, qr_ref, kn_ref, kr_ref, v_ref, o_ref, acc_ref, m_ref, *, blk):
    qi = pl.program_id(2)
    q = jnp.concatenate([qn_ref[...], qr_ref[...]], axis=1)
    ones = jnp.ones((blk, HEAD_DIM), BF16)
    acc_ref[...] = jnp.zeros_like(acc_ref)
    m_ref[...] = jnp.full_like(m_ref, NEG_BIG)

    def step(off, mask):
        k = jnp.concatenate([kn_ref[pl.ds(off, blk), :], kr_ref[pl.ds(off, blk), :]], axis=1)
        v = jnp.concatenate([v_ref[pl.ds(off, blk), :], ones], axis=1)
        s = lax.dot_general(q, k, (((1,), (1,)), ((), ())), preferred_element_type=F32)
        if mask is not None:
            s = jnp.where(mask, s, NEG_BIG)
        m_old = m_ref[...]
        m_new = jnp.maximum(m_old, jnp.max(s, axis=1, keepdims=True))
        p = jnp.exp(s - m_new)
        acc_ref[...] = acc_ref[...] * jnp.exp(m_old - m_new) + jnp.dot(
            p.astype(BF16), v, preferred_element_type=F32)
        m_ref[...] = m_new

    def body(j, _):
        step(pl.multiple_of(j * blk, blk), None)
        return 0

    lax.fori_loop(0, qi, body, 0)
    row_chunk = lax.broadcasted_iota(jnp.int32, (blk, blk), 0) // CHUNK
    col_chunk = lax.broadcasted_iota(jnp.int32, (blk, blk), 1) // CHUNK
    step(pl.multiple_of(qi * blk, blk), col_chunk <= row_chunk)
    acc = acc_ref[...]
    o_ref[...] = (acc[:, :HEAD_DIM] / acc[:, HEAD_DIM:]).astype(o_ref.dtype)


def _mla_attention(qn, qr, kn, kr, v, B, S):
    blk = _tile(S, 256)
    sh3 = lambda a: a.reshape(B, S, a.shape[-1])
    q_spec = pl.BlockSpec((None, blk, HEAD_DIM), lambda b, h, i: (b, i, h))
    kv_spec = pl.BlockSpec((None, S, HEAD_DIM), lambda b, h, i: (b, 0, h))
    kr_spec = pl.BlockSpec((None, S, LANE), lambda b, h, i: (b, 0, 0))
    return pl.pallas_call(
        functools.partial(_mla_kernel, blk=blk),
        grid=(B, HEADS, S // blk),
        in_specs=[q_spec, q_spec, kv_spec, kr_spec, kv_spec],
        out_specs=q_spec,
        out_shape=jax.ShapeDtypeStruct((B, S, ATT_WIDTH), BF16),
        scratch_shapes=[pltpu.VMEM((blk, 2 * HEAD_DIM), F32), pltpu.VMEM((blk, 1), F32)],
        compiler_params=_params(("parallel", "parallel", "arbitrary")),
        name="latent_attention",
    )(sh3(qn), sh3(qr), sh3(kn), sh3(kr), sh3(v))


def _sigmoid(x):
    return 1.0 / (1.0 + jnp.exp(-x))


def _merge_kernel(osb_ref, omla_ref, wsb_ref, wmla_ref, gsb_ref, gmla_ref, o_ref):
    a = jnp.dot(osb_ref[...], wsb_ref[...], preferred_element_type=F32)
    b = jnp.dot(omla_ref[...], wmla_ref[...], preferred_element_type=F32)
    o_ref[...] = (_sigmoid(gsb_ref[...].astype(F32)) * a
                  + _sigmoid(gmla_ref[...].astype(F32)) * b).astype(o_ref.dtype)


def _merge(o_sb, o_mla, w_sb, w_mla, proj):
    T, W = o_sb.shape
    D = w_sb.shape[1]
    tm, tn = _tile(T, 512), _tile(D, 1024)
    gate0 = COL_GATE // tn
    act = pl.BlockSpec((tm, W), lambda i, j: (i, 0))
    wsp = pl.BlockSpec((W, tn), lambda i, j: (0, j))
    return pl.pallas_call(
        _merge_kernel,
        grid=(T // tm, D // tn),
        in_specs=[act, act, wsp, wsp,
                  pl.BlockSpec((tm, tn), lambda i, j: (i, gate0 + j)),
                  pl.BlockSpec((tm, tn), lambda i, j: (i, gate0 + D // tn + j))],
        out_specs=pl.BlockSpec((tm, tn), lambda i, j: (i, j)),
        out_shape=jax.ShapeDtypeStruct((T, D), BF16),
        compiler_params=_params(("parallel", "arbitrary")),
        name="gated_branch_merge",
    )(o_sb, o_mla, w_sb, w_mla, proj, proj)


def _mlp_kernel(h_ref, wu_ref, wd_ref, o_ref):
    f = pl.program_id(1)
    u = jnp.dot(h_ref[...], wu_ref[...], preferred_element_type=F32)
    a = jnp.square(jnp.maximum(u, 0.0)).astype(BF16)
    d = jnp.dot(a, wd_ref[...], preferred_element_type=F32)

    @pl.when(f == 0)
    def _():
        o_ref[...] = d

    @pl.when(f > 0)
    def _():
        o_ref[...] += d


def _mlp(h, w_up, w_down):
    T, D = h.shape
    F = w_up.shape[1]
    tm, tf = _tile(T, 512), _tile(F, 512)
    return pl.pallas_call(
        _mlp_kernel,
        grid=(T // tm, F // tf),
        in_specs=[pl.BlockSpec((tm, D), lambda i, f: (i, 0)),
                  pl.BlockSpec((D, tf), lambda i, f: (0, f)),
                  pl.BlockSpec((tf, D), lambda i, f: (f, 0))],
        out_specs=pl.BlockSpec((tm, D), lambda i, f: (i, 0)),
        out_shape=jax.ShapeDtypeStruct((T, D), F32),
        compiler_params=_params(("parallel", "arbitrary")),
        name="relu2_mlp",
    )(h, w_up, w_down)


def _prep_w_in(w, D):
    pad = jnp.zeros((D, COL_GATE - (COL_KVD + KV_RANK + ROPE_DIM)), w.dtype)
    split = COL_KVD + KV_RANK + ROPE_DIM
    return jnp.concatenate([w[:, :split], pad, w[:, split:]], axis=1).astype(BF16)


def _prep_w_uq(w):
    w = w.reshape(Q_RANK, HEADS, HEAD_DIM + ROPE_DIM)
    nope = w[:, :, :HEAD_DIM].reshape(Q_RANK, ATT_WIDTH)
    rope = jnp.pad(w[:, :, HEAD_DIM:], ((0, 0), (0, 0), (0, LANE - ROPE_DIM))).reshape(Q_RANK, HEADS * LANE)
    return jnp.concatenate([nope, rope], axis=1).astype(BF16)


def _prep_w_ukv(w):
    w = w.reshape(KV_RANK, HEADS, 2 * HEAD_DIM)
    k = w[:, :, :HEAD_DIM].reshape(KV_RANK, ATT_WIDTH)
    v = w[:, :, HEAD_DIM:].reshape(KV_RANK, ATT_WIDTH)
    return jnp.concatenate([k, v], axis=1).astype(BF16)


def kernel(x, c, positions, w_ada, b_ada, g_pre_mix, g_post_mix, g_pre_mlp, g_post_mlp, w_in,
           g_q_lora, w_uq, g_kv_lora, w_ukv, w_o_sb, w_o_mla, w_out, w_up, w_down):
    B, S, D = x.shape
    L = w_ada.shape[0]
    T = B * S
    assert S % CHUNK == 0 and D % 512 == 0

    rows = 16
    c_pad = jnp.zeros((rows, D), F32).at[:B].set(c)
    ada = _ada(c_pad, w_ada, b_ada)[:, :B].reshape(L, B, 6, 1, D)
    cos_t, sin_t = _rope_tables(positions)

    x2d = x.reshape(T, D)
    h = _prenorm(x2d, g_pre_mix[0], ada[0], 0, 1, S)
    for l in range(L):
        proj = _matmul(h, _prep_w_in(w_in[l], D), BF16, "input_projection")
        o_sb = _sb_attention(proj.reshape(B, S, -1)).reshape(T, ATT_WIDTH)
        qn, qr, kn, v, kr = _mla_prep(proj, g_q_lora[l], g_kv_lora[l], _prep_w_uq(w_uq[l]),
                                      _prep_w_ukv(w_ukv[l]), cos_t, sin_t)
        o_mla = _mla_attention(qn, qr, kn, kr, v, B, S).reshape(T, ATT_WIDTH)
        merged = _merge(o_sb, o_mla, w_o_sb[l].astype(BF16), w_o_mla[l].astype(BF16), proj)
        y = _matmul(merged, w_out[l].astype(BF16), F32, "output_projection")
        x2d, h = _post(y, x2d, g_post_mix[l], ada[l], 2, S, nxt=(g_pre_mlp[l], ada[l], 3, 4))
        y = _mlp(h, w_up[l].astype(BF16), w_down[l].astype(BF16))
        nxt = (g_pre_mix[l + 1], ada[l + 1], 0, 1) if l + 1 < L else None
        x2d, h = _post(y, x2d, g_post_mlp[l], ada[l], 5, S, nxt=nxt)
    return x2d.reshape(B, S, D)
```

```python
import functools

import jax
import jax.numpy as jnp
from jax import lax
from jax.experimental import pallas as pl
from jax.experimental.pallas import tpu as pltpu

F32 = jnp.float32
BF16 = jnp.bfloat16

HEADS = 16
HEAD_DIM = 128
ATT_WIDTH = HEADS * HEAD_DIM
ROPE_DIM = 64
Q_RANK = 1024
KV_RANK = 512
CHUNK = 64
ROPE_THETA = 10000.0
EPS = 1e-6
NEG_BIG = -1e30
LOG2E = 1.4426950408889634
SB_HEAD_GROUP = 4
MLA_HEAD_GROUP = 8

COL_Q, COL_K, COL_V = 0, ATT_WIDTH, 2 * ATT_WIDTH
COL_QD = 3 * ATT_WIDTH
COL_KVD = COL_QD + Q_RANK
COL_GATE = COL_KVD + 1024

V7X_VMEM_LIMIT_BYTES = 56 * 1024 * 1024
LANE = 128


def _params(sem):
    return pltpu.CompilerParams(dimension_semantics=sem, vmem_limit_bytes=V7X_VMEM_LIMIT_BYTES)


def _tile(n, pref):
    if n <= pref:
        return n
    t = (pref // LANE) * LANE
    while t >= LANE:
        if n % t == 0:
            return t
        t -= LANE
    raise ValueError(f"no tile for {n}")


def _ada_kernel(c_ref, w_ref, b_ref, o_ref):
    c = c_ref[...]
    act = c * (1.0 / (1.0 + jnp.exp(-c)))
    o_ref[...] = jnp.dot(act.astype(BF16), w_ref[...].astype(BF16),
                         preferred_element_type=F32) + b_ref[...]


def _ada(c_pad, w_ada, b_ada):
    L, D, N = w_ada.shape
    R = c_pad.shape[0]
    tn = _tile(N, 512)
    return pl.pallas_call(
        _ada_kernel,
        grid=(L, N // tn),
        in_specs=[pl.BlockSpec((R, D), lambda l, j: (0, 0)),
                  pl.BlockSpec((None, D, tn), lambda l, j: (l, 0, j)),
                  pl.BlockSpec((None, 1, tn), lambda l, j: (l, 0, j))],
        out_specs=pl.BlockSpec((None, R, tn), lambda l, j: (l, 0, j)),
        out_shape=jax.ShapeDtypeStruct((L, R, N), F32),
        compiler_params=_params(("parallel", "parallel")),
        name="ada_proj",
    )(c_pad, w_ada, b_ada.reshape(L, 1, N))


def _rope_table_kernel(pos_ref, inv_ref, sgn_ref, cos_ref, sin_ref):
    ang = pos_ref[...].astype(F32) * inv_ref[...]
    cos_ref[...] = jnp.cos(ang)
    sin_ref[...] = jnp.sin(ang) * sgn_ref[...]


def _rope_tables(positions):
    T = positions.size
    half = ROPE_DIM // 2
    inv_freq = ROPE_THETA ** (-jnp.arange(half, dtype=F32) / half)
    zeros = jnp.zeros((LANE - ROPE_DIM,), F32)
    inv = jnp.concatenate([inv_freq, inv_freq, zeros]).reshape(1, LANE)
    sgn = jnp.concatenate([-jnp.ones((half,), F32), jnp.ones((half,), F32), zeros]).reshape(1, LANE)
    tm = _tile(T, 1024)
    tab = jax.ShapeDtypeStruct((T, LANE), F32)
    return pl.pallas_call(
        _rope_table_kernel,
        grid=(T // tm,),
        in_specs=[pl.BlockSpec((tm, 1), lambda i: (i, 0)),
                  pl.BlockSpec((1, LANE), lambda i: (0, 0)),
                  pl.BlockSpec((1, LANE), lambda i: (0, 0))],
        out_specs=[pl.BlockSpec((tm, LANE), lambda i: (i, 0))] * 2,
        out_shape=[tab, tab],
        compiler_params=_params(("parallel",)),
        name="rope_tables",
    )(positions.reshape(T, 1), inv, sgn)


def _rms(x, g):
    return x * lax.rsqrt(jnp.mean(x * x, axis=-1, keepdims=True) + EPS) * g


def _ada_spec(D, j, tiles_per_batch):
    return pl.BlockSpec((None, None, 1, D), lambda i: (i // tiles_per_batch, j, 0, 0))


def _prenorm_kernel(x_ref, g_ref, sh_ref, sc_ref, h_ref):
    h = _rms(x_ref[...], g_ref[...])
    h_ref[...] = (h * (1.0 + sc_ref[...]) + sh_ref[...]).astype(BF16)


def _prenorm(x2d, g, ada_l, j_shift, j_scale, S):
    T, D = x2d.shape
    tm = _tile(S, 512)
    tpb = S // tm
    row = pl.BlockSpec((tm, D), lambda i: (i, 0))
    vec = pl.BlockSpec((1, D), lambda i: (0, 0))
    return pl.pallas_call(
        _prenorm_kernel,
        grid=(T // tm,),
        in_specs=[row, vec, _ada_spec(D, j_shift, tpb), _ada_spec(D, j_scale, tpb)],
        out_specs=row,
        out_shape=jax.ShapeDtypeStruct((T, D), BF16),
        compiler_params=_params(("parallel",)),
        name="prenorm",
    )(x2d, g.reshape(1, D), ada_l, ada_l)


def _post_kernel(y_ref, x_ref, gpost_ref, gate_ref, *rest, with_next):
    y = y_ref[...].astype(F32)
    xn = x_ref[...] + gate_ref[...] * _rms(y, gpost_ref[...])
    if with_next:
        gpre_ref, sh_ref, sc_ref, xo_ref, h_ref = rest
        h = _rms(xn, gpre_ref[...])
        h_ref[...] = (h * (1.0 + sc_ref[...]) + sh_ref[...]).astype(BF16)
    else:
        (xo_ref,) = rest
    xo_ref[...] = xn


def _post(y, x2d, g_post, ada_l, j_gate, S, nxt=None):
    T, D = x2d.shape
    tm = _tile(S, 256)
    tpb = S // tm
    row = pl.BlockSpec((tm, D), lambda i: (i, 0))
    vec = pl.BlockSpec((1, D), lambda i: (0, 0))
    in_specs = [row, row, vec, _ada_spec(D, j_gate, tpb)]
    args = [y, x2d, g_post.reshape(1, D), ada_l]
    out_specs = [row]
    out_shape = [jax.ShapeDtypeStruct((T, D), F32)]
    if nxt is not None:
        g_pre, ada_n, j_shift, j_scale = nxt
        in_specs += [vec, _ada_spec(D, j_shift, tpb), _ada_spec(D, j_scale, tpb)]
        args += [g_pre.reshape(1, D), ada_n, ada_n]
        out_specs.append(row)
        out_shape.append(jax.ShapeDtypeStruct((T, D), BF16))
    outs = pl.pallas_call(
        functools.partial(_post_kernel, with_next=nxt is not None),
        grid=(T // tm,),
        in_specs=in_specs,
        out_specs=out_specs,
        out_shape=out_shape,
        compiler_params=_params(("parallel",)),
        name="post_norm_residual",
    )(*args)
    return outs if nxt is not None else (outs[0], None)


def _mm_kernel(x_ref, w_ref, o_ref):
    o_ref[...] = jnp.dot(x_ref[...], w_ref[...], preferred_element_type=F32).astype(o_ref.dtype)


def _matmul(x, w, out_dtype, name):
    M, K = x.shape
    N = w.shape[1]
    tm, tn = _tile(M, 1024), _tile(N, 1024)
    return pl.pallas_call(
        _mm_kernel,
        grid=(M // tm, N // tn),
        in_specs=[pl.BlockSpec((tm, K), lambda i, j: (i, 0)),
                  pl.BlockSpec((K, tn), lambda i, j: (0, j))],
        out_specs=pl.BlockSpec((tm, tn), lambda i, j: (i, j)),
        out_shape=jax.ShapeDtypeStruct((M, N), out_dtype),
        compiler_params=_params(("parallel", "arbitrary")),
        name=name,
    )(x, w)


def _sb_kernel(q_ref, k_ref, v_ref, o_ref, acc_ref, carry_ref, *, blk, heads):
    qi = pl.program_id(2)
    row = lax.broadcasted_iota(jnp.int32, (blk, blk), 0)
    col = lax.broadcasted_iota(jnp.int32, (blk, blk), 1)
    neg_suffix = jnp.where(row >= col, -1.0, 0.0).astype(BF16)
    strict = col < row

    def lanes(g):
        return slice(g * HEAD_DIM, (g + 1) * HEAD_DIM)

    def scores(off, g):
        k = k_ref[pl.ds(off, blk), lanes(g)]
        z = lax.dot_general(q_ref[:, lanes(g)], k, (((1,), (1,)), ((), ())),
                            preferred_element_type=F32)
        softplus = jnp.maximum(z, 0.0) + jnp.log(1.0 + jnp.exp2(jnp.abs(z) * (-LOG2E)))
        return z, softplus

    def weighted_values(off, g, w):
        return jnp.dot(w.astype(BF16), v_ref[pl.ds(off, blk), lanes(g)], preferred_element_type=F32)

    def suffix_sums(sp):
        return jnp.dot(sp.astype(BF16), neg_suffix, preferred_element_type=F32)

    hs = range(heads)

    off = pl.multiple_of(qi * blk, blk)
    zs = [scores(off, g) for g in hs]
    sps = [jnp.where(strict, sp, 0.0) for _, sp in zs]
    css = [suffix_sums(sp) for sp in sps]
    ws = [jnp.where(strict, jnp.exp2((z + cs) * LOG2E), 0.0) for (z, _), cs in zip(zs, css)]
    for g in hs:
        acc_ref[g] = weighted_values(off, g, ws[g])
        carry_ref[g] = -jnp.sum(sps[g], axis=1, keepdims=True)

    def body(n, _):
        off = pl.multiple_of((qi - 1 - n) * blk, blk)
        zs = [scores(off, g) for g in hs]
        css = [suffix_sums(sp) for _, sp in zs]
        ws = [jnp.exp2((z + cs + carry_ref[g]) * LOG2E) for g, ((z, _), cs) in enumerate(zip(zs, css))]
        for g in hs:
            acc_ref[g] += weighted_values(off, g, ws[g])
            carry_ref[g] -= jnp.sum(zs[g][1], axis=1, keepdims=True)
        return 0

    lax.fori_loop(0, qi, body, 0)
    for g in range(heads):
        o_ref[:, lanes(g)] = acc_ref[g].astype(o_ref.dtype)


def _sb_attention(proj3d):
    B, S, _ = proj3d.shape
    blk = _tile(S, 256)
    G = SB_HEAD_GROUP
    gw = G * HEAD_DIM
    q_spec = pl.BlockSpec((None, blk, gw), lambda b, h, i: (b, i, COL_Q // gw + h))
    k_spec = pl.BlockSpec((None, S, gw), lambda b, h, i: (b, 0, COL_K // gw + h))
    v_spec = pl.BlockSpec((None, S, gw), lambda b, h, i: (b, 0, COL_V // gw + h))
    return pl.pallas_call(
        functools.partial(_sb_kernel, blk=blk, heads=G),
        grid=(B, HEADS // G, S // blk),
        in_specs=[q_spec, k_spec, v_spec],
        out_specs=pl.BlockSpec((None, blk, gw), lambda b, h, i: (b, i, h)),
        out_shape=jax.ShapeDtypeStruct((B, S, ATT_WIDTH), BF16),
        scratch_shapes=[pltpu.VMEM((G, blk, HEAD_DIM), F32), pltpu.VMEM((G, blk, 1), F32)],
        compiler_params=_params(("parallel", "parallel", "arbitrary")),
        name="stick_breaking_attention",
    )(proj3d, proj3d, proj3d)


def _rope_rotate(xs, cos, sin_signed):
    half = ROPE_DIM // 2
    swapped = pltpu.roll(xs, half, 1) + pltpu.roll(xs, LANE - half, 1)
    return xs * cos + swapped * sin_signed


def _mla_prep_kernel(qd_ref, kvd_ref, gq_ref, gkv_ref, wq_ref, wkv_ref, cos_ref, sin_ref,
                     qn_ref, qr_ref, kn_ref, v_ref, kr_ref, *, scale):
    cos = cos_ref[...]
    sin = sin_ref[...]
    c_q = _rms(qd_ref[...].astype(F32), gq_ref[...]).astype(BF16)
    yq = jnp.dot(c_q, wq_ref[...], preferred_element_type=F32)
    qn_ref[...] = (yq[:, :ATT_WIDTH] * scale).astype(BF16)
    for h in range(HEADS):
        lo = ATT_WIDTH + h * LANE
        qr_ref[:, h * LANE:(h + 1) * LANE] = (
            _rope_rotate(yq[:, lo:lo + LANE], cos, sin) * scale).astype(BF16)
    kvd = kvd_ref[...].astype(F32)
    c_kv = _rms(kvd[:, :KV_RANK], gkv_ref[...]).astype(BF16)
    ykv = jnp.dot(c_kv, wkv_ref[...], preferred_element_type=F32)
    kn_ref[...] = ykv[:, :ATT_WIDTH].astype(BF16)
    v_ref[...] = ykv[:, ATT_WIDTH:].astype(BF16)
    kr_ref[...] = _rope_rotate(kvd[:, KV_RANK:KV_RANK + LANE], cos, sin).astype(BF16)


def _mla_prep(proj, g_q, g_kv, wq, wkv, cos_t, sin_t):
    T = proj.shape[0]
    tm = _tile(T, 256)
    row = lambda w, c: pl.BlockSpec((tm, w), lambda i: (i, c))
    full = lambda a: pl.BlockSpec(a.shape, lambda i: (0, 0))
    wide = jax.ShapeDtypeStruct((T, ATT_WIDTH), BF16)
    return pl.pallas_call(
        functools.partial(_mla_prep_kernel, scale=LOG2E * (HEAD_DIM + ROPE_DIM) ** -0.5),
        grid=(T // tm,),
        in_specs=[row(1024, COL_QD // 1024), row(1024, COL_KVD // 1024),
                  pl.BlockSpec((1, Q_RANK), lambda i: (0, 0)),
                  pl.BlockSpec((1, KV_RANK), lambda i: (0, 0)),
                  full(wq), full(wkv), row(LANE, 0), row(LANE, 0)],
        out_specs=[row(ATT_WIDTH, 0)] * 4 + [row(LANE, 0)],
        out_shape=[wide, wide, wide, wide, jax.ShapeDtypeStruct((T, LANE), BF16)],
        compiler_params=_params(("parallel",)),
        name="latent_projections",
    )(proj, proj, g_q.reshape(1, Q_RANK), g_kv.reshape(1, KV_RANK), wq, wkv, cos_t, sin_t)


def _mla_kernel(qn_ref, qr_ref, kn_ref, kr_ref, v_ref, o_ref, acc_ref, m_ref, *, blk, heads):
    qi = pl.program_id(2)
    ones = jnp.ones((blk, HEAD_DIM), BF16)
    acc_ref[...] = jnp.zeros_like(acc_ref)
    m_ref[...] = jnp.full_like(m_ref, NEG_BIG)

    def lanes(g):
        return slice(g * HEAD_DIM, (g + 1) * HEAD_DIM)

    def scores(off, kr, g):
        q = jnp.concatenate([qn_ref[:, lanes(g)], qr_ref[:, lanes(g)]], axis=1)
        k = jnp.concatenate([kn_ref[pl.ds(off, blk), lanes(g)], kr], axis=1)
        return lax.dot_general(q, k, (((1,), (1,)), ((), ())), preferred_element_type=F32)

    def step(off, mask):
        kr = kr_ref[pl.ds(off, blk), :]
        ss = [scores(off, kr, g) for g in range(heads)]
        if mask is not None:
            ss = [jnp.where(mask, s, NEG_BIG) for s in ss]
        ps, alphas = [], []
        for g, s in enumerate(ss):
            m_old = m_ref[g]
            m_new = jnp.maximum(m_old, jnp.max(s, axis=1, keepdims=True))
            ps.append(jnp.exp2(s - m_new).astype(BF16))
            alphas.append(jnp.exp2(m_old - m_new))
            m_ref[g] = m_new
        for g in range(heads):
            v = jnp.concatenate([v_ref[pl.ds(off, blk), lanes(g)], ones], axis=1)
            acc_ref[g] = acc_ref[g] * alphas[g] + jnp.dot(ps[g], v, preferred_element_type=F32)

    def body(j, _):
        step(pl.multiple_of(j * blk, blk), None)
        return 0

    lax.fori_loop(0, qi, body, 0)
    row_chunk = lax.broadcasted_iota(jnp.int32, (blk, blk), 0) // CHUNK
    col_chunk = lax.broadcasted_iota(jnp.int32, (blk, blk), 1) // CHUNK
    step(pl.multiple_of(qi * blk, blk), col_chunk <= row_chunk)
    for g in range(heads):
        acc = acc_ref[g]
        o_ref[:, lanes(g)] = (acc[:, :HEAD_DIM] / acc[:, HEAD_DIM:]).astype(o_ref.dtype)


def _mla_attention(qn, qr, kn, kr, v, B, S):
    blk = _tile(S, 256)
    G = MLA_HEAD_GROUP
    gw = G * HEAD_DIM
    sh3 = lambda a: a.reshape(B, S, a.shape[-1])
    q_spec = pl.BlockSpec((None, blk, gw), lambda b, h, i: (b, i, h))
    kv_spec = pl.BlockSpec((None, S, gw), lambda b, h, i: (b, 0, h))
    kr_spec = pl.BlockSpec((None, S, LANE), lambda b, h, i: (b, 0, 0))
    return pl.pallas_call(
        functools.partial(_mla_kernel, blk=blk, heads=G),
        grid=(B, HEADS // G, S // blk),
        in_specs=[q_spec, q_spec, kv_spec, kr_spec, kv_spec],
        out_specs=q_spec,
        out_shape=jax.ShapeDtypeStruct((B, S, ATT_WIDTH), BF16),
        scratch_shapes=[pltpu.VMEM((G, blk, 2 * HEAD_DIM), F32), pltpu.VMEM((G, blk, 1), F32)],
        compiler_params=_params(("parallel", "parallel", "arbitrary")),
        name="latent_attention",
    )(sh3(qn), sh3(qr), sh3(kn), sh3(kr), sh3(v))


def _sigmoid(x):
    return 1.0 / (1.0 + jnp.exp(-x))


def _merge_kernel(osb_ref, omla_ref, wsb_ref, wmla_ref, gsb_ref, gmla_ref, o_ref):
    a = jnp.dot(osb_ref[...], wsb_ref[...], preferred_element_type=F32)
    b = jnp.dot(omla_ref[...], wmla_ref[...], preferred_element_type=F32)
    o_ref[...] = (_sigmoid(gsb_ref[...].astype(F32)) * a
                  + _sigmoid(gmla_ref[...].astype(F32)) * b).astype(o_ref.dtype)


def _merge(o_sb, o_mla, w_sb, w_mla, proj):
    T, W = o_sb.shape
    D = w_sb.shape[1]
    tm, tn = _tile(T, 512), _tile(D, 1024)
    gate0 = COL_GATE // tn
    act = pl.BlockSpec((tm, W), lambda i, j: (i, 0))
    wsp = pl.BlockSpec((W, tn), lambda i, j: (0, j))
    return pl.pallas_call(
        _merge_kernel,
        grid=(T // tm, D // tn),
        in_specs=[act, act, wsp, wsp,
                  pl.BlockSpec((tm, tn), lambda i, j: (i, gate0 + j)),
                  pl.BlockSpec((tm, tn), lambda i, j: (i, gate0 + D // tn + j))],
        out_specs=pl.BlockSpec((tm, tn), lambda i, j: (i, j)),
        out_shape=jax.ShapeDtypeStruct((T, D), BF16),
        compiler_params=_params(("parallel", "arbitrary")),
        name="gated_branch_merge",
    )(o_sb, o_mla, w_sb, w_mla, proj, proj)


def _mlp_kernel(h_ref, wu_ref, wd_ref, o_ref):
    @pl.when(pl.program_id(1) == 0)
    def _():
        o_ref[...] = jnp.zeros_like(o_ref)

    u = jnp.dot(h_ref[...], wu_ref[...], preferred_element_type=F32)
    a = jnp.square(jnp.maximum(u, 0.0)).astype(BF16)
    o_ref[...] += jnp.dot(a, wd_ref[...], preferred_element_type=F32)


def _mlp(h, w_up, w_down):
    T, D = h.shape
    F = w_up.shape[1]
    tm, tf = _tile(T, 512), _tile(F, 512)
    return pl.pallas_call(
        _mlp_kernel,
        grid=(T // tm, F // tf),
        in_specs=[pl.BlockSpec((tm, D), lambda i, f: (i, 0)),
                  pl.BlockSpec((D, tf), lambda i, f: (0, f)),
                  pl.BlockSpec((tf, D), lambda i, f: (f, 0))],
        out_specs=pl.BlockSpec((tm, D), lambda i, f: (i, 0)),
        out_shape=jax.ShapeDtypeStruct((T, D), F32),
        compiler_params=_params(("parallel", "arbitrary")),
        name="relu2_mlp",
    )(h, w_up, w_down)


def _prep_w_in(w, D):
    pad = jnp.zeros((D, COL_GATE - (COL_KVD + KV_RANK + ROPE_DIM)), w.dtype)
    split = COL_KVD + KV_RANK + ROPE_DIM
    q = w[:, :ATT_WIDTH] * (HEAD_DIM ** -0.5)
    return jnp.concatenate([q, w[:, ATT_WIDTH:split], pad, w[:, split:]], axis=1).astype(BF16)


def _prep_w_uq(w):
    w = w.reshape(Q_RANK, HEADS, HEAD_DIM + ROPE_DIM)
    nope = w[:, :, :HEAD_DIM].reshape(Q_RANK, ATT_WIDTH)
    rope = jnp.pad(w[:, :, HEAD_DIM:], ((0, 0), (0, 0), (0, LANE - ROPE_DIM))).reshape(Q_RANK, HEADS * LANE)
    return jnp.concatenate([nope, rope], axis=1).astype(BF16)


def _prep_w_ukv(w):
    w = w.reshape(KV_RANK, HEADS, 2 * HEAD_DIM)
    k = w[:, :, :HEAD_DIM].reshape(KV_RANK, ATT_WIDTH)
    v = w[:, :, HEAD_DIM:].reshape(KV_RANK, ATT_WIDTH)
    return jnp.concatenate([k, v], axis=1).astype(BF16)


def kernel(x, c, positions, w_ada, b_ada, g_pre_mix, g_post_mix, g_pre_mlp, g_post_mlp, w_in,
           g_q_lora, w_uq, g_kv_lora, w_ukv, w_o_sb, w_o_mla, w_out, w_up, w_down):
    B, S, D = x.shape
    L = w_ada.shape[0]
    T = B * S
    assert S % CHUNK == 0 and D % 512 == 0

    rows = 16
    c_pad = jnp.zeros((rows, D), F32).at[:B].set(c)
    ada = _ada(c_pad, w_ada, b_ada)[:, :B].reshape(L, B, 6, 1, D)
    cos_t, sin_t = _rope_tables(positions)

    x2d = x.reshape(T, D)
    h = _prenorm(x2d, g_pre_mix[0], ada[0], 0, 1, S)
    for l in range(L):
        proj = _matmul(h, _prep_w_in(w_in[l], D), BF16, "input_projection")
        o_sb = _sb_attention(proj.reshape(B, S, -1)).reshape(T, ATT_WIDTH)
        qn, qr, kn, v, kr = _mla_prep(proj, g_q_lora[l], g_kv_lora[l], _prep_w_uq(w_uq[l]),
                                      _prep_w_ukv(w_ukv[l]), cos_t, sin_t)
        o_mla = _mla_attention(qn, qr, kn, kr, v, B, S).reshape(T, ATT_WIDTH)
        merged = _merge(o_sb, o_mla, w_o_sb[l].astype(BF16), w_o_mla[l].astype(BF16), proj)
        y = _matmul(merged, w_out[l].astype(BF16), F32, "output_projection")
        x2d, h = _post(y, x2d, g_post_mix[l], ada[l], 2, S, nxt=(g_pre_mlp[l], ada[l], 3, 4))
        y = _mlp(h, w_up[l].astype(BF16), w_down[l].astype(BF16))
        nxt = (g_pre_mix[l + 1], ada[l + 1], 0, 1) if l + 1 < L else None
        x2d, h = _post(y, x2d, g_post_mlp[l], ada[l], 5, S, nxt=nxt)
    return x2d.reshape(B, S, D)
```

```python
import functools

import jax
import jax.numpy as jnp
from jax import lax
from jax.experimental import pallas as pl
from jax.experimental.pallas import tpu as pltpu

F32 = jnp.float32
BF16 = jnp.bfloat16

HEADS = 16
HEAD_DIM = 128
ATT_WIDTH = HEADS * HEAD_DIM
ROPE_DIM = 64
Q_RANK = 1024
KV_RANK = 512
CHUNK = 64
ROPE_THETA = 10000.0
EPS = 1e-6
NEG_BIG = -1e30
LOG2E = 1.4426950408889634
DEAD_LOG_MASS = -104.0
SB_HEAD_GROUP = 4
MLA_HEAD_GROUP = 8

COL_Q, COL_K, COL_V = 0, ATT_WIDTH, 2 * ATT_WIDTH
COL_QD = 3 * ATT_WIDTH
COL_KVD = COL_QD + Q_RANK
COL_GATE = COL_KVD + 1024

V7X_VMEM_LIMIT_BYTES = 56 * 1024 * 1024
LANE = 128


def _params(sem):
    return pltpu.CompilerParams(dimension_semantics=sem, vmem_limit_bytes=V7X_VMEM_LIMIT_BYTES)


def _tile(n, pref):
    if n <= pref:
        return n
    t = (pref // LANE) * LANE
    while t >= LANE:
        if n % t == 0:
            return t
        t -= LANE
    raise ValueError(f"no tile for {n}")


def _ada_kernel(c_ref, w_ref, b_ref, o_ref):
    @pl.when(pl.program_id(1) == 0)
    def _():
        o_ref[...] = jnp.broadcast_to(b_ref[...], o_ref.shape)

    c = c_ref[...]
    act = c * (1.0 / (1.0 + jnp.exp(-c)))
    o_ref[...] += jnp.dot(act.astype(BF16), w_ref[...].astype(BF16), preferred_element_type=F32)


def _ada(c_pad, w_ada, b_ada):
    L, D, N = w_ada.shape
    R = c_pad.shape[0]
    tk = _tile(D, 128)
    return pl.pallas_call(
        _ada_kernel,
        grid=(L, D // tk),
        in_specs=[pl.BlockSpec((R, tk), lambda l, k: (0, k)),
                  pl.BlockSpec((None, tk, N), lambda l, k: (l, k, 0)),
                  pl.BlockSpec((None, 1, N), lambda l, k: (l, 0, 0))],
        out_specs=pl.BlockSpec((None, R, N), lambda l, k: (l, 0, 0)),
        out_shape=jax.ShapeDtypeStruct((L, R, N), F32),
        compiler_params=_params(("parallel", "arbitrary")),
        name="ada_proj",
    )(c_pad, w_ada, b_ada.reshape(L, 1, N))


def _rope_table_kernel(pos_ref, inv_ref, sgn_ref, cos_ref, sin_ref):
    ang = pos_ref[...].astype(F32) * inv_ref[...]
    cos_ref[...] = jnp.cos(ang)
    sin_ref[...] = jnp.sin(ang) * sgn_ref[...]


def _rope_tables(positions):
    T = positions.size
    half = ROPE_DIM // 2
    inv_freq = ROPE_THETA ** (-jnp.arange(half, dtype=F32) / half)
    zeros = jnp.zeros((LANE - ROPE_DIM,), F32)
    inv = jnp.concatenate([inv_freq, inv_freq, zeros]).reshape(1, LANE)
    sgn = jnp.concatenate([-jnp.ones((half,), F32), jnp.ones((half,), F32), zeros]).reshape(1, LANE)
    tm = _tile(T, 1024)
    tab = jax.ShapeDtypeStruct((T, LANE), F32)
    return pl.pallas_call(
        _rope_table_kernel,
        grid=(T // tm,),
        in_specs=[pl.BlockSpec((tm, 1), lambda i: (i, 0)),
                  pl.BlockSpec((1, LANE), lambda i: (0, 0)),
                  pl.BlockSpec((1, LANE), lambda i: (0, 0))],
        out_specs=[pl.BlockSpec((tm, LANE), lambda i: (i, 0))] * 2,
        out_shape=[tab, tab],
        compiler_params=_params(("parallel",)),
        name="rope_tables",
    )(positions.reshape(T, 1), inv, sgn)


def _rms(x, g):
    return x * lax.rsqrt(jnp.mean(x * x, axis=-1, keepdims=True) + EPS) * g


def _ada_spec(D, j, tiles_per_batch):
    return pl.BlockSpec((None, None, 1, D), lambda i: (i // tiles_per_batch, j, 0, 0))


def _prenorm_kernel(x_ref, g_ref, sh_ref, sc_ref, h_ref):
    h = _rms(x_ref[...], g_ref[...])
    h_ref[...] = (h * (1.0 + sc_ref[...]) + sh_ref[...]).astype(BF16)


def _prenorm(x2d, g, ada_l, j_shift, j_scale, S):
    T, D = x2d.shape
    tm = _tile(S, 512)
    tpb = S // tm
    row = pl.BlockSpec((tm, D), lambda i: (i, 0))
    vec = pl.BlockSpec((1, D), lambda i: (0, 0))
    return pl.pallas_call(
        _prenorm_kernel,
        grid=(T // tm,),
        in_specs=[row, vec, _ada_spec(D, j_shift, tpb), _ada_spec(D, j_scale, tpb)],
        out_specs=row,
        out_shape=jax.ShapeDtypeStruct((T, D), BF16),
        compiler_params=_params(("parallel",)),
        name="prenorm",
    )(x2d, g.reshape(1, D), ada_l, ada_l)


def _post_kernel(y_ref, x_ref, gpost_ref, gate_ref, *rest, with_next):
    y = y_ref[...].astype(F32)
    xn = x_ref[...] + gate_ref[...] * _rms(y, gpost_ref[...])
    if with_next:
        gpre_ref, sh_ref, sc_ref, xo_ref, h_ref = rest
        h = _rms(xn, gpre_ref[...])
        h_ref[...] = (h * (1.0 + sc_ref[...]) + sh_ref[...]).astype(BF16)
    else:
        (xo_ref,) = rest
    xo_ref[...] = xn


def _post(y, x2d, g_post, ada_l, j_gate, S, nxt=None):
    T, D = x2d.shape
    tm = _tile(S, 256)
    tpb = S // tm
    row = pl.BlockSpec((tm, D), lambda i: (i, 0))
    vec = pl.BlockSpec((1, D), lambda i: (0, 0))
    in_specs = [row, row, vec, _ada_spec(D, j_gate, tpb)]
    args = [y, x2d, g_post.reshape(1, D), ada_l]
    out_specs = [row]
    out_shape = [jax.ShapeDtypeStruct((T, D), F32)]
    if nxt is not None:
        g_pre, ada_n, j_shift, j_scale = nxt
        in_specs += [vec, _ada_spec(D, j_shift, tpb), _ada_spec(D, j_scale, tpb)]
        args += [g_pre.reshape(1, D), ada_n, ada_n]
        out_specs.append(row)
        out_shape.append(jax.ShapeDtypeStruct((T, D), BF16))
    outs = pl.pallas_call(
        functools.partial(_post_kernel, with_next=nxt is not None),
        grid=(T // tm,),
        in_specs=in_specs,
        out_specs=out_specs,
        out_shape=out_shape,
        compiler_params=_params(("parallel",)),
        name="post_norm_residual",
    )(*args)
    return outs if nxt is not None else (outs[0], None)


def _mm_kernel(x_ref, w_ref, o_ref):
    o_ref[...] = jnp.dot(x_ref[...], w_ref[...], preferred_element_type=F32).astype(o_ref.dtype)


def _matmul(x, w, layer, out_dtype, name):
    M, K = x.shape
    N = w.shape[2]
    tm, tn = _tile(M, 1024), _tile(N, 1024)
    return pl.pallas_call(
        _mm_kernel,
        grid=(M // tm, N // tn),
        in_specs=[pl.BlockSpec((tm, K), lambda i, j: (i, 0)),
                  pl.BlockSpec((None, K, tn), lambda i, j: (layer, 0, j))],
        out_specs=pl.BlockSpec((tm, tn), lambda i, j: (i, j)),
        out_shape=jax.ShapeDtypeStruct((M, N), out_dtype),
        compiler_params=_params(("parallel", "arbitrary")),
        name=name,
    )(x, w)


def _sb_kernel(q_ref, k_ref, v_ref, wf_ref, o_ref, wb_ref, acc_ref, carry_ref, *, blk, heads):
    wb_ref[...] = wf_ref[...].astype(BF16)
    qi = pl.program_id(2)
    row = lax.broadcasted_iota(jnp.int32, (blk, blk), 0)
    col = lax.broadcasted_iota(jnp.int32, (blk, blk), 1)
    neg_suffix = jnp.where(row >= col, -1.0, 0.0).astype(BF16)
    strict = col < row

    def lanes(g):
        return slice(g * HEAD_DIM, (g + 1) * HEAD_DIM)

    def scores(off, g):
        k = k_ref[pl.ds(off, blk), lanes(g)]
        z = lax.dot_general(q_ref[:, lanes(g)], k, (((1,), (1,)), ((), ())),
                            preferred_element_type=F32)
        softplus = jnp.maximum(z, 0.0) + jnp.log(1.0 + jnp.exp2(jnp.abs(z) * (-LOG2E)))
        return z, softplus

    def weighted_values(off, g, w):
        return jnp.dot(w.astype(BF16), v_ref[pl.ds(off, blk), lanes(g)], preferred_element_type=F32)

    def suffix_sums(sp):
        return jnp.dot(sp.astype(BF16), neg_suffix, preferred_element_type=F32)

    hs = range(heads)

    off = pl.multiple_of(qi * blk, blk)
    zs = [scores(off, g) for g in hs]
    sps = [jnp.where(strict, sp, 0.0) for _, sp in zs]
    css = [suffix_sums(sp) for sp in sps]
    ws = [jnp.where(strict, jnp.exp2((z + cs) * LOG2E), 0.0) for (z, _), cs in zip(zs, css)]
    for g in hs:
        acc_ref[g] = weighted_values(off, g, ws[g])
        carry_ref[g] = -jnp.sum(sps[g], axis=1, keepdims=True)

    def body(state):
        n, _ = state
        off = pl.multiple_of((qi - 1 - n) * blk, blk)
        zs = [scores(off, g) for g in hs]
        css = [suffix_sums(sp) for _, sp in zs]
        ws = [jnp.exp2((z + cs + carry_ref[g]) * LOG2E) for g, ((z, _), cs) in enumerate(zip(zs, css))]
        for g in hs:
            acc_ref[g] += weighted_values(off, g, ws[g])
            carry_ref[g] -= jnp.sum(zs[g][1], axis=1, keepdims=True)
        return n + 1, jnp.max(carry_ref[...]) > DEAD_LOG_MASS

    lax.while_loop(lambda st: jnp.logical_and(st[0] < qi, st[1]), body, (jnp.int32(0), True))
    for g in range(heads):
        o_ref[:, lanes(g)] = acc_ref[g].astype(o_ref.dtype)


def _slab_specs(w, layer, grid):
    _, K, N = w.shape
    steps = grid[0] * grid[1] * grid[2]
    rows = K // steps
    assert rows * steps == K and rows % 16 == 0
    index = lambda b, h, i: (layer, (b * grid[1] + h) * grid[2] + i, 0)
    spec = pl.BlockSpec((None, rows, N), index)
    return spec, pl.BlockSpec((rows, N), lambda b, h, i: index(b, h, i)[1:]), jax.ShapeDtypeStruct((K, N), BF16)


def _sb_attention(proj3d, w_cast, layer):
    B, S, _ = proj3d.shape
    blk = _tile(S, 256)
    G = SB_HEAD_GROUP
    gw = G * HEAD_DIM
    grid = (B, HEADS // G, S // blk)
    wf_spec, wb_spec, wb_shape = _slab_specs(w_cast, layer, grid)
    q_spec = pl.BlockSpec((None, blk, gw), lambda b, h, i: (b, i, COL_Q // gw + h))
    k_spec = pl.BlockSpec((None, S, gw), lambda b, h, i: (b, 0, COL_K // gw + h))
    v_spec = pl.BlockSpec((None, S, gw), lambda b, h, i: (b, 0, COL_V // gw + h))
    return pl.pallas_call(
        functools.partial(_sb_kernel, blk=blk, heads=G),
        grid=grid,
        in_specs=[q_spec, k_spec, v_spec, wf_spec],
        out_specs=[pl.BlockSpec((None, blk, gw), lambda b, h, i: (b, i, h)), wb_spec],
        out_shape=[jax.ShapeDtypeStruct((B, S, ATT_WIDTH), BF16), wb_shape],
        scratch_shapes=[pltpu.VMEM((G, blk, HEAD_DIM), F32), pltpu.VMEM((G, blk, 1), F32)],
        compiler_params=_params(("parallel", "parallel", "arbitrary")),
        name="stick_breaking_attention",
    )(proj3d, proj3d, proj3d, w_cast)


def _rope_rotate(xs, cos, sin_signed):
    half = ROPE_DIM // 2
    swapped = pltpu.roll(xs, half, 1) + pltpu.roll(xs, LANE - half, 1)
    return xs * cos + swapped * sin_signed


def _mla_prep_kernel(qd_ref, kvd_ref, gq_ref, gkv_ref, wq_ref, wkv_ref, cos_ref, sin_ref,
                     qn_ref, qr_ref, kn_ref, v_ref, kr_ref, *, scale):
    cos = cos_ref[...]
    sin = sin_ref[...]
    c_q = _rms(qd_ref[...].astype(F32), gq_ref[...]).astype(BF16)
    yq = jnp.dot(c_q, wq_ref[...], preferred_element_type=F32)
    qn_ref[...] = (yq[:, :ATT_WIDTH] * scale).astype(BF16)
    for h in range(HEADS):
        lo = ATT_WIDTH + h * LANE
        qr_ref[:, h * LANE:(h + 1) * LANE] = (
            _rope_rotate(yq[:, lo:lo + LANE], cos, sin) * scale).astype(BF16)
    kvd = kvd_ref[...].astype(F32)
    c_kv = _rms(kvd[:, :KV_RANK], gkv_ref[...]).astype(BF16)
    ykv = jnp.dot(c_kv, wkv_ref[...], preferred_element_type=F32)
    kn_ref[...] = ykv[:, :ATT_WIDTH].astype(BF16)
    v_ref[...] = ykv[:, ATT_WIDTH:].astype(BF16)
    kr_ref[...] = _rope_rotate(kvd[:, KV_RANK:KV_RANK + LANE], cos, sin).astype(BF16)


def _mla_prep(proj, g_q, g_kv, wq, wkv, cos_t, sin_t):
    T = proj.shape[0]
    tm = _tile(T, 256)
    row = lambda w, c: pl.BlockSpec((tm, w), lambda i: (i, c))
    full = lambda a: pl.BlockSpec(a.shape, lambda i: (0, 0))
    wide = jax.ShapeDtypeStruct((T, ATT_WIDTH), BF16)
    return pl.pallas_call(
        functools.partial(_mla_prep_kernel, scale=LOG2E * (HEAD_DIM + ROPE_DIM) ** -0.5),
        grid=(T // tm,),
        in_specs=[row(1024, COL_QD // 1024), row(1024, COL_KVD // 1024),
                  pl.BlockSpec((1, Q_RANK), lambda i: (0, 0)),
                  pl.BlockSpec((1, KV_RANK), lambda i: (0, 0)),
                  full(wq), full(wkv), row(LANE, 0), row(LANE, 0)],
        out_specs=[row(ATT_WIDTH, 0)] * 4 + [row(LANE, 0)],
        out_shape=[wide, wide, wide, wide, jax.ShapeDtypeStruct((T, LANE), BF16)],
        compiler_params=_params(("parallel",)),
        name="latent_projections",
    )(proj, proj, g_q.reshape(1, Q_RANK), g_kv.reshape(1, KV_RANK), wq, wkv, cos_t, sin_t)


def _mla_kernel(qn_ref, qr_ref, kn_ref, kr_ref, v_ref, wf_ref, o_ref, wb_ref, acc_ref, m_ref, *, blk, heads):
    wb_ref[...] = wf_ref[...].astype(BF16)
    qi = pl.program_id(2)
    acc_ref[...] = jnp.zeros_like(acc_ref)
    m_ref[...] = jnp.full_like(m_ref, NEG_BIG)

    def lanes(g):
        return slice(g * HEAD_DIM, (g + 1) * HEAD_DIM)

    def step(off, width, mask):
        kr = kr_ref[pl.ds(off, width), :]
        ones = jnp.ones((width, HEAD_DIM), BF16)
        ss = []
        for g in range(heads):
            q = jnp.concatenate([qn_ref[:, lanes(g)], qr_ref[:, lanes(g)]], axis=1)
            k = jnp.concatenate([kn_ref[pl.ds(off, width), lanes(g)], kr], axis=1)
            s = lax.dot_general(q, k, (((1,), (1,)), ((), ())), preferred_element_type=F32)
            ss.append(s if mask is None else jnp.where(mask, s, NEG_BIG))
        ps, alphas = [], []
        for g, s in enumerate(ss):
            m_old = m_ref[g]
            m_new = jnp.maximum(m_old, jnp.max(s, axis=1, keepdims=True))
            ps.append(jnp.exp2(s - jnp.tile(m_new, (1, width // LANE))).astype(BF16))
            alphas.append(jnp.exp2(m_old - m_new))
            m_ref[g] = m_new
        for g in range(heads):
            v = jnp.concatenate([v_ref[pl.ds(off, width), lanes(g)], ones], axis=1)
            acc_ref[g] = acc_ref[g] * jnp.tile(alphas[g], (1, 2)) + jnp.dot(
                ps[g], v, preferred_element_type=F32)

    def body(j, _):
        step(pl.multiple_of(j * (2 * blk), 2 * blk), 2 * blk, None)
        return 0

    lax.fori_loop(0, qi // 2, body, 0)

    @pl.when(qi % 2 == 1)
    def _():
        step(pl.multiple_of((qi - 1) * blk, blk), blk, None)

    row_chunk = lax.broadcasted_iota(jnp.int32, (blk, blk), 0) // CHUNK
    col_chunk = lax.broadcasted_iota(jnp.int32, (blk, blk), 1) // CHUNK
    step(pl.multiple_of(qi * blk, blk), blk, col_chunk <= row_chunk)
    for g in range(heads):
        acc = acc_ref[g]
        o_ref[:, lanes(g)] = (acc[:, :HEAD_DIM] / acc[:, HEAD_DIM:]).astype(o_ref.dtype)


def _mla_attention(qn, qr, kn, kr, v, B, S, w_cast, layer):
    blk = _tile(S, 256)
    G = MLA_HEAD_GROUP
    gw = G * HEAD_DIM
    grid = (B, HEADS // G, S // blk)
    wf_spec, wb_spec, wb_shape = _slab_specs(w_cast, layer, grid)
    sh3 = lambda a: a.reshape(B, S, a.shape[-1])
    once = pl.Buffered(1)
    q_spec = pl.BlockSpec((None, blk, gw), lambda b, h, i: (b, i, h))
    kv_spec = pl.BlockSpec((None, S, gw), lambda b, h, i: (b, 0, h), pipeline_mode=once)
    kr_spec = pl.BlockSpec((None, S, LANE), lambda b, h, i: (b, 0, 0), pipeline_mode=once)
    return pl.pallas_call(
        functools.partial(_mla_kernel, blk=blk, heads=G),
        grid=grid,
        in_specs=[q_spec, q_spec, kv_spec, kr_spec, kv_spec, wf_spec],
        out_specs=[q_spec, wb_spec],
        out_shape=[jax.ShapeDtypeStruct((B, S, ATT_WIDTH), BF16), wb_shape],
        scratch_shapes=[pltpu.VMEM((G, blk, 2 * HEAD_DIM), F32), pltpu.VMEM((G, blk, LANE), F32)],
        compiler_params=_params(("parallel", "parallel", "arbitrary")),
        name="latent_attention",
    )(sh3(qn), sh3(qr), sh3(kn), sh3(kr), sh3(v), w_cast)


def _sigmoid(x):
    return 1.0 / (1.0 + jnp.exp(-x))


def _merge_kernel(osb_ref, omla_ref, wsb_ref, wmla_ref, gsb_ref, gmla_ref, o_ref):
    a = jnp.dot(osb_ref[...], wsb_ref[...], preferred_element_type=F32)
    b = jnp.dot(omla_ref[...], wmla_ref[...], preferred_element_type=F32)
    o_ref[...] = (_sigmoid(gsb_ref[...].astype(F32)) * a
                  + _sigmoid(gmla_ref[...].astype(F32)) * b).astype(o_ref.dtype)


def _merge(o_sb, o_mla, w_sb, w_mla, layer, proj):
    T, W = o_sb.shape
    D = w_sb.shape[2]
    tm, tn = _tile(T, 512), _tile(D, 1024)
    gate0 = COL_GATE // tn
    act = pl.BlockSpec((tm, W), lambda i, j: (i, 0))
    wsp = pl.BlockSpec((None, W, tn), lambda i, j: (layer, 0, j))
    return pl.pallas_call(
        _merge_kernel,
        grid=(T // tm, D // tn),
        in_specs=[act, act, wsp, wsp,
                  pl.BlockSpec((tm, tn), lambda i, j: (i, gate0 + j)),
                  pl.BlockSpec((tm, tn), lambda i, j: (i, gate0 + D // tn + j))],
        out_specs=pl.BlockSpec((tm, tn), lambda i, j: (i, j)),
        out_shape=jax.ShapeDtypeStruct((T, D), BF16),
        compiler_params=_params(("parallel", "arbitrary")),
        name="gated_branch_merge",
    )(o_sb, o_mla, w_sb, w_mla, proj, proj)


def _mlp_kernel(h_ref, wu_ref, wd_ref, o_ref):
    @pl.when(pl.program_id(1) == 0)
    def _():
        o_ref[...] = jnp.zeros_like(o_ref)

    u = jnp.dot(h_ref[...], wu_ref[...], preferred_element_type=F32)
    a = jnp.square(jnp.maximum(u, 0.0)).astype(BF16)
    o_ref[...] += jnp.dot(a, wd_ref[...], preferred_element_type=F32)


def _mlp(h, w_up, w_down):
    T, D = h.shape
    F = w_up.shape[1]
    tm, tf = _tile(T, 512), _tile(F, 512)
    return pl.pallas_call(
        _mlp_kernel,
        grid=(T // tm, F // tf),
        in_specs=[pl.BlockSpec((tm, D), lambda i, f: (i, 0)),
                  pl.BlockSpec((D, tf), lambda i, f: (0, f)),
                  pl.BlockSpec((tf, D), lambda i, f: (f, 0))],
        out_specs=pl.BlockSpec((tm, D), lambda i, f: (i, 0)),
        out_shape=jax.ShapeDtypeStruct((T, D), F32),
        compiler_params=_params(("parallel", "arbitrary")),
        name="relu2_mlp",
    )(h, w_up, w_down)


def _cast_kernel(w_ref, o_ref):
    o_ref[...] = w_ref[...].astype(BF16)


def _cast_bf16(w):
    L, K, N = w.shape
    tn = _tile(N, 4096)
    tk = _tile(K, max(LANE, (2 * 1024 * 1024) // tn))
    spec = pl.BlockSpec((None, tk, tn), lambda l, i, j: (l, i, j))
    return pl.pallas_call(
        _cast_kernel,
        grid=(L, K // tk, N // tn),
        in_specs=[spec],
        out_specs=spec,
        out_shape=jax.ShapeDtypeStruct((L, K, N), BF16),
        compiler_params=_params(("parallel", "parallel", "parallel")),
        name="cast_weights",
    )(w)


W_IN_BLOCK = 1024
KVD_WIDTH = KV_RANK + ROPE_DIM
GATE_BLOCK0 = COL_GATE // W_IN_BLOCK


def _w_in_kernel(a_ref, b_ref, o_ref, *, q_scale):
    j = pl.program_id(2)
    n_q = ATT_WIDTH // W_IN_BLOCK

    @pl.when(j < n_q)
    def _():
        o_ref[...] = (a_ref[...] * q_scale).astype(BF16)

    @pl.when((j >= n_q) & (j < GATE_BLOCK0 - 1))
    def _():
        o_ref[...] = a_ref[...].astype(BF16)

    @pl.when(j == GATE_BLOCK0 - 1)
    def _():
        lane = lax.broadcasted_iota(jnp.int32, a_ref.shape, 1)
        o_ref[...] = jnp.where(lane < KVD_WIDTH, a_ref[...], 0.0).astype(BF16)

    @pl.when(j >= GATE_BLOCK0)
    def _():
        lo = (KVD_WIDTH // LANE) * LANE
        x = jnp.concatenate([a_ref[:, lo:], b_ref[:, :W_IN_BLOCK - lo + LANE]], axis=1)
        shift = KVD_WIDTH - lo
        x = pltpu.roll(x, x.shape[1] - shift, 1)
        o_ref[...] = x[:, :W_IN_BLOCK].astype(BF16)


def _prep_w_in(w_in):
    L, D, _ = w_in.shape
    n_out = (COL_GATE + 2 * D) // W_IN_BLOCK
    tk = _tile(D, 1024)
    g0 = GATE_BLOCK0
    a_spec = pl.BlockSpec((None, tk, W_IN_BLOCK), lambda l, i, j: (l, i, jnp.where(j < g0, j, j - 1)))
    b_spec = pl.BlockSpec((None, tk, W_IN_BLOCK), lambda l, i, j: (l, i, jnp.where(j < g0, g0, j)))
    return pl.pallas_call(
        functools.partial(_w_in_kernel, q_scale=HEAD_DIM ** -0.5),
        grid=(L, D // tk, n_out),
        in_specs=[a_spec, b_spec],
        out_specs=pl.BlockSpec((None, tk, W_IN_BLOCK), lambda l, i, j: (l, i, j)),
        out_shape=jax.ShapeDtypeStruct((L, D, n_out * W_IN_BLOCK), BF16),
        compiler_params=_params(("parallel", "parallel", "arbitrary")),
        name="input_weight_layout",
    )(w_in, w_in)


def _prep_w_uq(w):
    w = w.reshape(Q_RANK, HEADS, HEAD_DIM + ROPE_DIM)
    nope = w[:, :, :HEAD_DIM].reshape(Q_RANK, ATT_WIDTH)
    rope = jnp.pad(w[:, :, HEAD_DIM:], ((0, 0), (0, 0), (0, LANE - ROPE_DIM))).reshape(Q_RANK, HEADS * LANE)
    return jnp.concatenate([nope, rope], axis=1).astype(BF16)


def _prep_w_ukv(w):
    w = w.reshape(KV_RANK, HEADS, 2 * HEAD_DIM)
    k = w[:, :, :HEAD_DIM].reshape(KV_RANK, ATT_WIDTH)
    v = w[:, :, HEAD_DIM:].reshape(KV_RANK, ATT_WIDTH)
    return jnp.concatenate([k, v], axis=1).astype(BF16)


def kernel(x, c, positions, w_ada, b_ada, g_pre_mix, g_post_mix, g_pre_mlp, g_post_mlp, w_in,
           g_q_lora, w_uq, g_kv_lora, w_ukv, w_o_sb, w_o_mla, w_out, w_up, w_down):
    B, S, D = x.shape
    L = w_ada.shape[0]
    T = B * S
    assert S % CHUNK == 0 and D % 512 == 0

    rows = 16
    c_pad = jnp.zeros((rows, D), F32).at[:B].set(c)
    ada = _ada(c_pad, w_ada, b_ada)[:, :B].reshape(L, B, 6, 1, D)
    cos_t, sin_t = _rope_tables(positions)

    w_in_b = _prep_w_in(w_in)
    w_o_sb_b, w_o_mla_b, w_out_b = _cast_bf16(w_o_sb), _cast_bf16(w_o_mla), _cast_bf16(w_out)

    x2d = x.reshape(T, D)
    h = _prenorm(x2d, g_pre_mix[0], ada[0], 0, 1, S)
    for l in range(L):
        proj = _matmul(h, w_in_b, l, BF16, "input_projection")
        o_sb, w_down_b = _sb_attention(proj.reshape(B, S, -1), w_down, l)
        o_sb = o_sb.reshape(T, ATT_WIDTH)
        qn, qr, kn, v, kr = _mla_prep(proj, g_q_lora[l], g_kv_lora[l], _prep_w_uq(w_uq[l]),
                                      _prep_w_ukv(w_ukv[l]), cos_t, sin_t)
        o_mla, w_up_b = _mla_attention(qn, qr, kn, kr, v, B, S, w_up, l)
        o_mla = o_mla.reshape(T, ATT_WIDTH)
        merged = _merge(o_sb, o_mla, w_o_sb_b, w_o_mla_b, l, proj)
        y = _matmul(merged, w_out_b, l, F32, "output_projection")
        x2d, h = _post(y, x2d, g_post_mix[l], ada[l], 2, S, nxt=(g_pre_mlp[l], ada[l], 3, 4))
        y = _mlp(h, w_up_b, w_down_b)
        nxt = (g_pre_mix[l + 1], ada[l + 1], 0, 1) if l + 1 < L else None
        x2d, h = _post(y, x2d, g_post_mlp[l], ada[l], 5, S, nxt=nxt)
    return x2d.reshape(B, S, D)
```

```python
import functools

import jax
import jax.numpy as jnp
from jax import lax
from jax.experimental import pallas as pl
from jax.experimental.pallas import tpu as pltpu

F32 = jnp.float32
BF16 = jnp.bfloat16

HEADS = 16
HEAD_DIM = 128
ATT_WIDTH = HEADS * HEAD_DIM
ROPE_DIM = 64
Q_RANK = 1024
KV_RANK = 512
CHUNK = 64
ROPE_THETA = 10000.0
EPS = 1e-6
NEG_BIG = -1e30
LOG2E = 1.4426950408889634
DEAD_LOG_MASS = -104.0
SB_HEAD_GROUP = 4
MLA_HEAD_GROUP = 8

COL_Q, COL_K, COL_V = 0, ATT_WIDTH, 2 * ATT_WIDTH
COL_QD = 3 * ATT_WIDTH
COL_KVD = COL_QD + Q_RANK
COL_GATE = COL_KVD + 1024

V7X_VMEM_LIMIT_BYTES = 56 * 1024 * 1024
LANE = 128


def _params(sem):
    return pltpu.CompilerParams(dimension_semantics=sem, vmem_limit_bytes=V7X_VMEM_LIMIT_BYTES)


def _tile(n, pref):
    if n <= pref:
        return n
    t = (pref // LANE) * LANE
    while t >= LANE:
        if n % t == 0:
            return t
        t -= LANE
    raise ValueError(f"no tile for {n}")


def _ada_kernel(c_ref, w_ref, b_ref, o_ref):
    @pl.when(pl.program_id(1) == 0)
    def _():
        o_ref[...] = jnp.broadcast_to(b_ref[...], o_ref.shape)

    c = c_ref[...]
    act = c * (1.0 / (1.0 + jnp.exp(-c)))
    o_ref[...] += jnp.dot(act.astype(BF16), w_ref[...].astype(BF16), preferred_element_type=F32)


def _ada(c_pad, w_ada, b_ada):
    L, D, N = w_ada.shape
    R = c_pad.shape[0]
    tk = _tile(D, 128)
    return pl.pallas_call(
        _ada_kernel,
        grid=(L, D // tk),
        in_specs=[pl.BlockSpec((R, tk), lambda l, k: (0, k)),
                  pl.BlockSpec((None, tk, N), lambda l, k: (l, k, 0)),
                  pl.BlockSpec((None, 1, N), lambda l, k: (l, 0, 0))],
        out_specs=pl.BlockSpec((None, R, N), lambda l, k: (l, 0, 0)),
        out_shape=jax.ShapeDtypeStruct((L, R, N), F32),
        compiler_params=_params(("parallel", "arbitrary")),
        name="ada_proj",
    )(c_pad, w_ada, b_ada.reshape(L, 1, N))


def _rope_table_kernel(pos_ref, inv_ref, sgn_ref, cos_ref, sin_ref):
    ang = pos_ref[...].astype(F32) * inv_ref[...]
    cos_ref[...] = jnp.cos(ang)
    sin_ref[...] = jnp.sin(ang) * sgn_ref[...]


def _rope_tables(positions):
    T = positions.size
    half = ROPE_DIM // 2
    inv_freq = ROPE_THETA ** (-jnp.arange(half, dtype=F32) / half)
    zeros = jnp.zeros((LANE - ROPE_DIM,), F32)
    inv = jnp.concatenate([inv_freq, inv_freq, zeros]).reshape(1, LANE)
    sgn = jnp.concatenate([-jnp.ones((half,), F32), jnp.ones((half,), F32), zeros]).reshape(1, LANE)
    tm = _tile(T, 1024)
    tab = jax.ShapeDtypeStruct((T, LANE), F32)
    return pl.pallas_call(
        _rope_table_kernel,
        grid=(T // tm,),
        in_specs=[pl.BlockSpec((tm, 1), lambda i: (i, 0)),
                  pl.BlockSpec((1, LANE), lambda i: (0, 0)),
                  pl.BlockSpec((1, LANE), lambda i: (0, 0))],
        out_specs=[pl.BlockSpec((tm, LANE), lambda i: (i, 0))] * 2,
        out_shape=[tab, tab],
        compiler_params=_params(("parallel",)),
        name="rope_tables",
    )(positions.reshape(T, 1), inv, sgn)


def _rms(x, g):
    return x * lax.rsqrt(jnp.mean(x * x, axis=-1, keepdims=True) + EPS) * g


def _ada_spec(D, j, tiles_per_batch):
    return pl.BlockSpec((None, None, 1, D), lambda i: (i // tiles_per_batch, j, 0, 0))


def _prenorm_kernel(x_ref, g_ref, sh_ref, sc_ref, h_ref):
    h = _rms(x_ref[...], g_ref[...])
    h_ref[...] = (h * (1.0 + sc_ref[...]) + sh_ref[...]).astype(BF16)


def _prenorm(x2d, g, ada_l, j_shift, j_scale, S):
    T, D = x2d.shape
    tm = _tile(S, 512)
    tpb = S // tm
    row = pl.BlockSpec((tm, D), lambda i: (i, 0))
    vec = pl.BlockSpec((1, D), lambda i: (0, 0))
    return pl.pallas_call(
        _prenorm_kernel,
        grid=(T // tm,),
        in_specs=[row, vec, _ada_spec(D, j_shift, tpb), _ada_spec(D, j_scale, tpb)],
        out_specs=row,
        out_shape=jax.ShapeDtypeStruct((T, D), BF16),
        compiler_params=_params(("parallel",)),
        name="prenorm",
    )(x2d, g.reshape(1, D), ada_l, ada_l)


def _post_kernel(y_ref, x_ref, gpost_ref, gate_ref, *rest, with_next):
    y = y_ref[...].astype(F32)
    xn = x_ref[...] + gate_ref[...] * _rms(y, gpost_ref[...])
    if with_next:
        gpre_ref, sh_ref, sc_ref, xo_ref, h_ref = rest
        h = _rms(xn, gpre_ref[...])
        h_ref[...] = (h * (1.0 + sc_ref[...]) + sh_ref[...]).astype(BF16)
    else:
        (xo_ref,) = rest
    xo_ref[...] = xn


def _post(y, x2d, g_post, ada_l, j_gate, S, nxt=None):
    T, D = x2d.shape
    tm = _tile(S, 256)
    tpb = S // tm
    row = pl.BlockSpec((tm, D), lambda i: (i, 0))
    vec = pl.BlockSpec((1, D), lambda i: (0, 0))
    in_specs = [row, row, vec, _ada_spec(D, j_gate, tpb)]
    args = [y, x2d, g_post.reshape(1, D), ada_l]
    out_specs = [row]
    out_shape = [jax.ShapeDtypeStruct((T, D), F32)]
    if nxt is not None:
        g_pre, ada_n, j_shift, j_scale = nxt
        in_specs += [vec, _ada_spec(D, j_shift, tpb), _ada_spec(D, j_scale, tpb)]
        args += [g_pre.reshape(1, D), ada_n, ada_n]
        out_specs.append(row)
        out_shape.append(jax.ShapeDtypeStruct((T, D), BF16))
    outs = pl.pallas_call(
        functools.partial(_post_kernel, with_next=nxt is not None),
        grid=(T // tm,),
        in_specs=in_specs,
        out_specs=out_specs,
        out_shape=out_shape,
        compiler_params=_params(("parallel",)),
        name="post_norm_residual",
    )(*args)
    return outs if nxt is not None else (outs[0], None)


def _mm_kernel(x_ref, w_ref, o_ref):
    o_ref[...] = jnp.dot(x_ref[...], w_ref[...], preferred_element_type=F32).astype(o_ref.dtype)


def _matmul(x, w, out_dtype, name):
    M, K = x.shape
    N = w.shape[1]
    tm, tn = _tile(M, 1024), _tile(N, 1024)
    return pl.pallas_call(
        _mm_kernel,
        grid=(M // tm, N // tn),
        in_specs=[pl.BlockSpec((tm, K), lambda i, j: (i, 0)),
                  pl.BlockSpec((K, tn), lambda i, j: (0, j))],
        out_specs=pl.BlockSpec((tm, tn), lambda i, j: (i, j)),
        out_shape=jax.ShapeDtypeStruct((M, N), out_dtype),
        compiler_params=_params(("parallel", "arbitrary")),
        name=name,
    )(x, w)


def _mm_nt_kernel(x_ref, wt_ref, o_ref):
    o_ref[...] = lax.dot_general(x_ref[...], wt_ref[...], (((1,), (1,)), ((), ())),
                                 preferred_element_type=F32).astype(o_ref.dtype)


def _matmul_nt(x, wt, layer, out_dtype, name):
    M, K = x.shape
    N = wt.shape[1]
    tm, tn = _tile(M, 1024), _tile(N, 1024)
    return pl.pallas_call(
        _mm_nt_kernel,
        grid=(M // tm, N // tn),
        in_specs=[pl.BlockSpec((tm, K), lambda i, j: (i, 0)),
                  pl.BlockSpec((None, tn, K), lambda i, j: (layer, j, 0))],
        out_specs=pl.BlockSpec((tm, tn), lambda i, j: (i, j)),
        out_shape=jax.ShapeDtypeStruct((M, N), out_dtype),
        compiler_params=_params(("parallel", "arbitrary")),
        name=name,
    )(x, wt)


def _sb_kernel(q_ref, k_ref, v_ref, *rest, blk, heads, n_cast):
    wf_refs, o_ref, wb_refs = rest[:n_cast], rest[n_cast], rest[n_cast + 1:2 * n_cast + 1]
    acc_ref, carry_ref = rest[2 * n_cast + 1:]
    for wf_ref, wb_ref in zip(wf_refs, wb_refs):
        wb_ref[...] = wf_ref[...].astype(BF16)
    qi = pl.program_id(2)
    row = lax.broadcasted_iota(jnp.int32, (blk, blk), 0)
    col = lax.broadcasted_iota(jnp.int32, (blk, blk), 1)
    neg_suffix = jnp.where(row >= col, -1.0, 0.0).astype(BF16)
    strict = col < row

    def lanes(g):
        return slice(g * HEAD_DIM, (g + 1) * HEAD_DIM)

    def scores(off, g):
        k = k_ref[pl.ds(off, blk), lanes(g)]
        z = lax.dot_general(q_ref[:, lanes(g)], k, (((1,), (1,)), ((), ())),
                            preferred_element_type=F32)
        softplus = jnp.maximum(z, 0.0) + jnp.log(1.0 + jnp.exp2(jnp.abs(z) * (-LOG2E)))
        return z, softplus

    def weighted_values(off, g, w):
        return jnp.dot(w.astype(BF16), v_ref[pl.ds(off, blk), lanes(g)], preferred_element_type=F32)

    def suffix_sums(sp):
        return jnp.dot(sp.astype(BF16), neg_suffix, preferred_element_type=F32)

    hs = range(heads)

    off = pl.multiple_of(qi * blk, blk)
    zs = [scores(off, g) for g in hs]
    sps = [jnp.where(strict, sp, 0.0) for _, sp in zs]
    css = [suffix_sums(sp) for sp in sps]
    ws = [jnp.where(strict, jnp.exp2((z + cs) * LOG2E), 0.0) for (z, _), cs in zip(zs, css)]
    for g in hs:
        acc_ref[g] = weighted_values(off, g, ws[g])
        carry_ref[g] = -jnp.sum(sps[g], axis=1, keepdims=True)

    def body(state):
        n, _ = state
        off = pl.multiple_of((qi - 1 - n) * blk, blk)
        zs = [scores(off, g) for g in hs]
        css = [suffix_sums(sp) for _, sp in zs]
        ws = [jnp.exp2((z + cs + carry_ref[g]) * LOG2E) for g, ((z, _), cs) in enumerate(zip(zs, css))]
        for g in hs:
            acc_ref[g] += weighted_values(off, g, ws[g])
            carry_ref[g] -= jnp.sum(zs[g][1], axis=1, keepdims=True)
        return n + 1, jnp.max(carry_ref[...]) > DEAD_LOG_MASS

    lax.while_loop(lambda st: jnp.logical_and(st[0] < qi, st[1]), body, (jnp.int32(0), True))
    for g in range(heads):
        o_ref[:, lanes(g)] = acc_ref[g].astype(o_ref.dtype)


def _slab_specs(w, layer, grid):
    _, K, N = w.shape
    steps = grid[0] * grid[1] * grid[2]
    rows = K // steps
    assert rows * steps == K and rows % 16 == 0
    index = lambda b, h, i: (layer, (b * grid[1] + h) * grid[2] + i, 0)
    spec = pl.BlockSpec((None, rows, N), index)
    return spec, pl.BlockSpec((rows, N), lambda b, h, i: index(b, h, i)[1:]), jax.ShapeDtypeStruct((K, N), BF16)


def _sb_attention(proj3d, w_casts, layer):
    B, S, _ = proj3d.shape
    blk = _tile(S, 256)
    G = SB_HEAD_GROUP
    gw = G * HEAD_DIM
    grid = (B, HEADS // G, S // blk)
    wf_specs, wb_specs, wb_shapes = zip(*[_slab_specs(w, layer, grid) for w in w_casts])
    q_spec = pl.BlockSpec((None, blk, gw), lambda b, h, i: (b, i, COL_Q // gw + h))
    k_spec = pl.BlockSpec((None, S, gw), lambda b, h, i: (b, 0, COL_K // gw + h))
    v_spec = pl.BlockSpec((None, S, gw), lambda b, h, i: (b, 0, COL_V // gw + h))
    outs = pl.pallas_call(
        functools.partial(_sb_kernel, blk=blk, heads=G, n_cast=len(w_casts)),
        grid=grid,
        in_specs=[q_spec, k_spec, v_spec, *wf_specs],
        out_specs=[pl.BlockSpec((None, blk, gw), lambda b, h, i: (b, i, h)), *wb_specs],
        out_shape=[jax.ShapeDtypeStruct((B, S, ATT_WIDTH), BF16), *wb_shapes],
        scratch_shapes=[pltpu.VMEM((G, blk, HEAD_DIM), F32), pltpu.VMEM((G, blk, 1), F32)],
        compiler_params=_params(("parallel", "parallel", "arbitrary")),
        name="stick_breaking_attention",
    )(proj3d, proj3d, proj3d, *w_casts)
    return outs[0], outs[1:]


def _rope_rotate(xs, cos, sin_signed):
    half = ROPE_DIM // 2
    swapped = pltpu.roll(xs, half, 1) + pltpu.roll(xs, LANE - half, 1)
    return xs * cos + swapped * sin_signed


def _mla_prep_kernel(qd_ref, kvd_ref, gq_ref, gkv_ref, wq_ref, wkv_ref, cos_ref, sin_ref,
                     qn_ref, qr_ref, kn_ref, v_ref, kr_ref, *, scale):
    cos = cos_ref[...]
    sin = sin_ref[...]
    c_q = _rms(qd_ref[...].astype(F32), gq_ref[...]).astype(BF16)
    yq = jnp.dot(c_q, wq_ref[...], preferred_element_type=F32)
    qn_ref[...] = (yq[:, :ATT_WIDTH] * scale).astype(BF16)
    for h in range(HEADS):
        lo = ATT_WIDTH + h * LANE
        qr_ref[:, h * LANE:(h + 1) * LANE] = (
            _rope_rotate(yq[:, lo:lo + LANE], cos, sin) * scale).astype(BF16)
    kvd = kvd_ref[...].astype(F32)
    c_kv = _rms(kvd[:, :KV_RANK], gkv_ref[...]).astype(BF16)
    ykv = jnp.dot(c_kv, wkv_ref[...], preferred_element_type=F32)
    kn_ref[...] = ykv[:, :ATT_WIDTH].astype(BF16)
    v_ref[...] = ykv[:, ATT_WIDTH:].astype(BF16)
    kr_ref[...] = _rope_rotate(kvd[:, KV_RANK:KV_RANK + LANE], cos, sin).astype(BF16)


def _mla_prep(proj, g_q, g_kv, wq, wkv, cos_t, sin_t):
    T = proj.shape[0]
    tm = _tile(T, 256)
    row = lambda w, c: pl.BlockSpec((tm, w), lambda i: (i, c))
    full = lambda a: pl.BlockSpec(a.shape, lambda i: (0, 0))
    wide = jax.ShapeDtypeStruct((T, ATT_WIDTH), BF16)
    return pl.pallas_call(
        functools.partial(_mla_prep_kernel, scale=LOG2E * (HEAD_DIM + ROPE_DIM) ** -0.5),
        grid=(T // tm,),
        in_specs=[row(1024, COL_QD // 1024), row(1024, COL_KVD // 1024),
                  pl.BlockSpec((1, Q_RANK), lambda i: (0, 0)),
                  pl.BlockSpec((1, KV_RANK), lambda i: (0, 0)),
                  full(wq), full(wkv), row(LANE, 0), row(LANE, 0)],
        out_specs=[row(ATT_WIDTH, 0)] * 4 + [row(LANE, 0)],
        out_shape=[wide, wide, wide, wide, jax.ShapeDtypeStruct((T, LANE), BF16)],
        compiler_params=_params(("parallel",)),
        name="latent_projections",
    )(proj, proj, g_q.reshape(1, Q_RANK), g_kv.reshape(1, KV_RANK), wq, wkv, cos_t, sin_t)


def _mla_kernel(qn_ref, qr_ref, kn_ref, kr_ref, v_ref, wf_ref, o_ref, wb_ref, acc_ref, m_ref, *, blk, heads):
    wb_ref[...] = wf_ref[...].astype(BF16)
    qi = pl.program_id(2)
    acc_ref[...] = jnp.zeros_like(acc_ref)
    m_ref[...] = jnp.full_like(m_ref, NEG_BIG)

    def lanes(g):
        return slice(g * HEAD_DIM, (g + 1) * HEAD_DIM)

    def step(off, width, mask):
        kr = kr_ref[pl.ds(off, width), :]
        ones = jnp.ones((width, HEAD_DIM), BF16)
        ss = []
        for g in range(heads):
            q = jnp.concatenate([qn_ref[:, lanes(g)], qr_ref[:, lanes(g)]], axis=1)
            k = jnp.concatenate([kn_ref[pl.ds(off, width), lanes(g)], kr], axis=1)
            s = lax.dot_general(q, k, (((1,), (1,)), ((), ())), preferred_element_type=F32)
            ss.append(s if mask is None else jnp.where(mask, s, NEG_BIG))
        ps, alphas = [], []
        for g, s in enumerate(ss):
            m_old = m_ref[g]
            m_new = jnp.maximum(m_old, jnp.max(s, axis=1, keepdims=True))
            ps.append(jnp.exp2(s - jnp.tile(m_new, (1, width // LANE))).astype(BF16))
            alphas.append(jnp.exp2(m_old - m_new))
            m_ref[g] = m_new
        for g in range(heads):
            v = jnp.concatenate([v_ref[pl.ds(off, width), lanes(g)], ones], axis=1)
            acc_ref[g] = acc_ref[g] * jnp.tile(alphas[g], (1, 2)) + jnp.dot(
                ps[g], v, preferred_element_type=F32)

    def body(j, _):
        step(pl.multiple_of(j * (2 * blk), 2 * blk), 2 * blk, None)
        return 0

    lax.fori_loop(0, qi // 2, body, 0)

    @pl.when(qi % 2 == 1)
    def _():
        step(pl.multiple_of((qi - 1) * blk, blk), blk, None)

    row_chunk = lax.broadcasted_iota(jnp.int32, (blk, blk), 0) // CHUNK
    col_chunk = lax.broadcasted_iota(jnp.int32, (blk, blk), 1) // CHUNK
    step(pl.multiple_of(qi * blk, blk), blk, col_chunk <= row_chunk)
    for g in range(heads):
        acc = acc_ref[g]
        o_ref[:, lanes(g)] = (acc[:, :HEAD_DIM] / acc[:, HEAD_DIM:]).astype(o_ref.dtype)


def _mla_attention(qn, qr, kn, kr, v, B, S, w_cast, layer):
    blk = _tile(S, 256)
    G = MLA_HEAD_GROUP
    gw = G * HEAD_DIM
    grid = (B, HEADS // G, S // blk)
    wf_spec, wb_spec, wb_shape = _slab_specs(w_cast, layer, grid)
    sh3 = lambda a: a.reshape(B, S, a.shape[-1])
    once = pl.Buffered(1)
    q_spec = pl.BlockSpec((None, blk, gw), lambda b, h, i: (b, i, h))
    kv_spec = pl.BlockSpec((None, S, gw), lambda b, h, i: (b, 0, h), pipeline_mode=once)
    kr_spec = pl.BlockSpec((None, S, LANE), lambda b, h, i: (b, 0, 0), pipeline_mode=once)
    return pl.pallas_call(
        functools.partial(_mla_kernel, blk=blk, heads=G),
        grid=grid,
        in_specs=[q_spec, q_spec, kv_spec, kr_spec, kv_spec, wf_spec],
        out_specs=[q_spec, wb_spec],
        out_shape=[jax.ShapeDtypeStruct((B, S, ATT_WIDTH), BF16), wb_shape],
        scratch_shapes=[pltpu.VMEM((G, blk, 2 * HEAD_DIM), F32), pltpu.VMEM((G, blk, LANE), F32)],
        compiler_params=_params(("parallel", "parallel", "arbitrary")),
        name="latent_attention",
    )(sh3(qn), sh3(qr), sh3(kn), sh3(kr), sh3(v), w_cast)


def _sigmoid(x):
    return 1.0 / (1.0 + jnp.exp(-x))


def _merge_kernel(osb_ref, omla_ref, wsb_ref, wmla_ref, gsb_ref, gmla_ref, o_ref):
    a = jnp.dot(osb_ref[...], wsb_ref[...], preferred_element_type=F32)
    b = jnp.dot(omla_ref[...], wmla_ref[...], preferred_element_type=F32)
    o_ref[...] = (_sigmoid(gsb_ref[...].astype(F32)) * a
                  + _sigmoid(gmla_ref[...].astype(F32)) * b).astype(o_ref.dtype)


def _merge(o_sb, o_mla, w_sb, w_mla, proj):
    T, W = o_sb.shape
    D = w_sb.shape[1]
    tm, tn = _tile(T, 512), _tile(D, 1024)
    gate0 = COL_GATE // tn
    act = pl.BlockSpec((tm, W), lambda i, j: (i, 0))
    wsp = pl.BlockSpec((W, tn), lambda i, j: (0, j))
    return pl.pallas_call(
        _merge_kernel,
        grid=(T // tm, D // tn),
        in_specs=[act, act, wsp, wsp,
                  pl.BlockSpec((tm, tn), lambda i, j: (i, gate0 + j)),
                  pl.BlockSpec((tm, tn), lambda i, j: (i, gate0 + D // tn + j))],
        out_specs=pl.BlockSpec((tm, tn), lambda i, j: (i, j)),
        out_shape=jax.ShapeDtypeStruct((T, D), BF16),
        compiler_params=_params(("parallel", "arbitrary")),
        name="gated_branch_merge",
    )(o_sb, o_mla, w_sb, w_mla, proj, proj)


def _mlp_kernel(h_ref, wu_ref, wd_ref, o_ref):
    @pl.when(pl.program_id(1) == 0)
    def _():
        o_ref[...] = jnp.zeros_like(o_ref)

    u = jnp.dot(h_ref[...], wu_ref[...], preferred_element_type=F32)
    a = jnp.square(jnp.maximum(u, 0.0)).astype(BF16)
    o_ref[...] += jnp.dot(a, wd_ref[...], preferred_element_type=F32)


def _mlp(h, w_up, w_down):
    T, D = h.shape
    F = w_up.shape[1]
    tm, tf = _tile(T, 512), _tile(F, 512)
    return pl.pallas_call(
        _mlp_kernel,
        grid=(T // tm, F // tf),
        in_specs=[pl.BlockSpec((tm, D), lambda i, f: (i, 0)),
                  pl.BlockSpec((D, tf), lambda i, f: (0, f)),
                  pl.BlockSpec((tf, D), lambda i, f: (f, 0))],
        out_specs=pl.BlockSpec((tm, D), lambda i, f: (i, 0)),
        out_shape=jax.ShapeDtypeStruct((T, D), F32),
        compiler_params=_params(("parallel", "arbitrary")),
        name="relu2_mlp",
    )(h, w_up, w_down)


W_IN_ROWS = 64
KVD_END = COL_KVD + KV_RANK + ROPE_DIM


def _w_in_kernel(w_ref, o_ref, *, q_scale):
    j = pl.program_id(1)

    @pl.when(j < ATT_WIDTH // W_IN_ROWS)
    def _():
        o_ref[...] = (w_ref[...] * q_scale).astype(BF16)

    @pl.when((j >= ATT_WIDTH // W_IN_ROWS) & ((j < KVD_END // W_IN_ROWS) | (j >= COL_GATE // W_IN_ROWS)))
    def _():
        o_ref[...] = w_ref[...].astype(BF16)

    @pl.when((j >= KVD_END // W_IN_ROWS) & (j < COL_GATE // W_IN_ROWS))
    def _():
        o_ref[...] = jnp.zeros_like(o_ref)


def _prep_w_in(w_in):
    L, D, _ = w_in.shape
    wt = jnp.swapaxes(w_in, 1, 2)
    n_out = (COL_GATE + 2 * D) // W_IN_ROWS
    pad_blocks = (COL_GATE - KVD_END) // W_IN_ROWS
    last_real = KVD_END // W_IN_ROWS - 1
    gate0 = COL_GATE // W_IN_ROWS

    def src(l, j):
        return l, jnp.where(j < gate0, jnp.minimum(j, last_real), j - pad_blocks), 0

    return pl.pallas_call(
        functools.partial(_w_in_kernel, q_scale=HEAD_DIM ** -0.5),
        grid=(L, n_out),
        in_specs=[pl.BlockSpec((None, W_IN_ROWS, D), src)],
        out_specs=pl.BlockSpec((None, W_IN_ROWS, D), lambda l, j: (l, j, 0)),
        out_shape=jax.ShapeDtypeStruct((L, n_out * W_IN_ROWS, D), BF16),
        compiler_params=_params(("parallel", "arbitrary")),
        name="input_weight_layout",
    )(wt)


def _prep_w_uq(w):
    w = w.reshape(Q_RANK, HEADS, HEAD_DIM + ROPE_DIM)
    nope = w[:, :, :HEAD_DIM].reshape(Q_RANK, ATT_WIDTH)
    rope = jnp.pad(w[:, :, HEAD_DIM:], ((0, 0), (0, 0), (0, LANE - ROPE_DIM))).reshape(Q_RANK, HEADS * LANE)
    return jnp.concatenate([nope, rope], axis=1).astype(BF16)


def _prep_w_ukv(w):
    w = w.reshape(KV_RANK, HEADS, 2 * HEAD_DIM)
    k = w[:, :, :HEAD_DIM].reshape(KV_RANK, ATT_WIDTH)
    v = w[:, :, HEAD_DIM:].reshape(KV_RANK, ATT_WIDTH)
    return jnp.concatenate([k, v], axis=1).astype(BF16)


def kernel(x, c, positions, w_ada, b_ada, g_pre_mix, g_post_mix, g_pre_mlp, g_post_mlp, w_in,
           g_q_lora, w_uq, g_kv_lora, w_ukv, w_o_sb, w_o_mla, w_out, w_up, w_down):
    B, S, D = x.shape
    L = w_ada.shape[0]
    T = B * S
    assert S % CHUNK == 0 and D % 512 == 0

    rows = 16
    c_pad = jnp.zeros((rows, D), F32).at[:B].set(c)
    ada = _ada(c_pad, w_ada, b_ada)[:, :B].reshape(L, B, 6, 1, D)
    cos_t, sin_t = _rope_tables(positions)

    w_in_b = _prep_w_in(w_in)

    x2d = x.reshape(T, D)
    h = _prenorm(x2d, g_pre_mix[0], ada[0], 0, 1, S)
    for l in range(L):
        proj = _matmul_nt(h, w_in_b, l, BF16, "input_projection")
        o_sb, (w_down_b, w_o_sb_b, w_o_mla_b, w_out_b) = _sb_attention(
            proj.reshape(B, S, -1), (w_down, w_o_sb, w_o_mla, w_out), l)
        o_sb = o_sb.reshape(T, ATT_WIDTH)
        qn, qr, kn, v, kr = _mla_prep(proj, g_q_lora[l], g_kv_lora[l], _prep_w_uq(w_uq[l]),
                                      _prep_w_ukv(w_ukv[l]), cos_t, sin_t)
        o_mla, w_up_b = _mla_attention(qn, qr, kn, kr, v, B, S, w_up, l)
        o_mla = o_mla.reshape(T, ATT_WIDTH)
        merged = _merge(o_sb, o_mla, w_o_sb_b, w_o_mla_b, proj)
        y = _matmul(merged, w_out_b, F32, "output_projection")
        x2d, h = _post(y, x2d, g_post_mix[l], ada[l], 2, S, nxt=(g_pre_mlp[l], ada[l], 3, 4))
        y = _mlp(h, w_up_b, w_down_b)
        nxt = (g_pre_mix[l + 1], ada[l + 1], 0, 1) if l + 1 < L else None
        x2d, h = _post(y, x2d, g_post_mlp[l], ada[l], 5, S, nxt=nxt)
    return x2d.reshape(B, S, D)
```

```python
import functools

import jax
import jax.numpy as jnp
from jax import lax
from jax.experimental import pallas as pl
from jax.experimental.pallas import tpu as pltpu

F32 = jnp.float32
BF16 = jnp.bfloat16

HEADS = 16
HEAD_DIM = 128
ATT_WIDTH = HEADS * HEAD_DIM
ROPE_DIM = 64
Q_RANK = 1024
KV_RANK = 512
CHUNK = 64
ROPE_THETA = 10000.0
EPS = 1e-6
NEG_BIG = -1e30
LOG2E = 1.4426950408889634
DEAD_LOG_MASS = -104.0
SB_HEAD_GROUP = 4
MLA_HEAD_GROUP = 8

COL_Q, COL_K, COL_V = 0, ATT_WIDTH, 2 * ATT_WIDTH
COL_QD = 3 * ATT_WIDTH
COL_KVD = COL_QD + Q_RANK
COL_GATE = COL_KVD + 1024

V7X_VMEM_LIMIT_BYTES = 56 * 1024 * 1024
LANE = 128


def _params(sem):
    return pltpu.CompilerParams(dimension_semantics=sem, vmem_limit_bytes=V7X_VMEM_LIMIT_BYTES)


def _tile(n, pref):
    if n <= pref:
        return n
    t = (pref // LANE) * LANE
    while t >= LANE:
        if n % t == 0:
            return t
        t -= LANE
    raise ValueError(f"no tile for {n}")


def _ada_kernel(c_ref, w_ref, b_ref, o_ref):
    @pl.when(pl.program_id(1) == 0)
    def _():
        o_ref[...] = jnp.broadcast_to(b_ref[...], o_ref.shape)

    c = c_ref[...]
    act = c * (1.0 / (1.0 + jnp.exp(-c)))
    o_ref[...] += jnp.dot(act.astype(BF16), w_ref[...].astype(BF16), preferred_element_type=F32)


def _ada(c_pad, w_ada, b_ada):
    L, D, N = w_ada.shape
    R = c_pad.shape[0]
    tk = _tile(D, 128)
    return pl.pallas_call(
        _ada_kernel,
        grid=(L, D // tk),
        in_specs=[pl.BlockSpec((R, tk), lambda l, k: (0, k)),
                  pl.BlockSpec((None, tk, N), lambda l, k: (l, k, 0)),
                  pl.BlockSpec((None, 1, N), lambda l, k: (l, 0, 0))],
        out_specs=pl.BlockSpec((None, R, N), lambda l, k: (l, 0, 0)),
        out_shape=jax.ShapeDtypeStruct((L, R, N), F32),
        compiler_params=_params(("parallel", "arbitrary")),
        name="ada_proj",
    )(c_pad, w_ada, b_ada.reshape(L, 1, N))


def _rope_table_kernel(pos_ref, inv_ref, sgn_ref, cos_ref, sin_ref):
    ang = pos_ref[...].astype(F32) * inv_ref[...]
    cos_ref[...] = jnp.cos(ang)
    sin_ref[...] = jnp.sin(ang) * sgn_ref[...]


def _rope_tables(positions):
    T = positions.size
    half = ROPE_DIM // 2
    inv_freq = ROPE_THETA ** (-jnp.arange(half, dtype=F32) / half)
    zeros = jnp.zeros((LANE - ROPE_DIM,), F32)
    inv = jnp.concatenate([inv_freq, inv_freq, zeros]).reshape(1, LANE)
    sgn = jnp.concatenate([-jnp.ones((half,), F32), jnp.ones((half,), F32), zeros]).reshape(1, LANE)
    tm = _tile(T, 1024)
    tab = jax.ShapeDtypeStruct((T, LANE), F32)
    return pl.pallas_call(
        _rope_table_kernel,
        grid=(T // tm,),
        in_specs=[pl.BlockSpec((tm, 1), lambda i: (i, 0)),
                  pl.BlockSpec((1, LANE), lambda i: (0, 0)),
                  pl.BlockSpec((1, LANE), lambda i: (0, 0))],
        out_specs=[pl.BlockSpec((tm, LANE), lambda i: (i, 0))] * 2,
        out_shape=[tab, tab],
        compiler_params=_params(("parallel",)),
        name="rope_tables",
    )(positions.reshape(T, 1), inv, sgn)


def _rms(x, g):
    return x * lax.rsqrt(jnp.mean(x * x, axis=-1, keepdims=True) + EPS) * g


def _ada_spec(D, j, tiles_per_batch):
    return pl.BlockSpec((None, None, 1, D), lambda i: (i // tiles_per_batch, j, 0, 0))


def _prenorm_kernel(x_ref, g_ref, sh_ref, sc_ref, h_ref):
    h = _rms(x_ref[...], g_ref[...])
    h_ref[...] = (h * (1.0 + sc_ref[...]) + sh_ref[...]).astype(BF16)


def _prenorm(x2d, g, ada_l, j_shift, j_scale, S):
    T, D = x2d.shape
    tm = _tile(S, 512)
    tpb = S // tm
    row = pl.BlockSpec((tm, D), lambda i: (i, 0))
    vec = pl.BlockSpec((1, D), lambda i: (0, 0))
    return pl.pallas_call(
        _prenorm_kernel,
        grid=(T // tm,),
        in_specs=[row, vec, _ada_spec(D, j_shift, tpb), _ada_spec(D, j_scale, tpb)],
        out_specs=row,
        out_shape=jax.ShapeDtypeStruct((T, D), BF16),
        compiler_params=_params(("parallel",)),
        name="prenorm",
    )(x2d, g.reshape(1, D), ada_l, ada_l)


def _post_kernel(y_ref, x_ref, gpost_ref, gate_ref, *rest, with_next):
    y = y_ref[...].astype(F32)
    xn = x_ref[...] + gate_ref[...] * _rms(y, gpost_ref[...])
    if with_next:
        gpre_ref, sh_ref, sc_ref, xo_ref, h_ref = rest
        h = _rms(xn, gpre_ref[...])
        h_ref[...] = (h * (1.0 + sc_ref[...]) + sh_ref[...]).astype(BF16)
    else:
        (xo_ref,) = rest
    xo_ref[...] = xn


def _post(y, x2d, g_post, ada_l, j_gate, S, nxt=None):
    T, D = x2d.shape
    tm = _tile(S, 256)
    tpb = S // tm
    row = pl.BlockSpec((tm, D), lambda i: (i, 0))
    vec = pl.BlockSpec((1, D), lambda i: (0, 0))
    in_specs = [row, row, vec, _ada_spec(D, j_gate, tpb)]
    args = [y, x2d, g_post.reshape(1, D), ada_l]
    out_specs = [row]
    out_shape = [jax.ShapeDtypeStruct((T, D), F32)]
    if nxt is not None:
        g_pre, ada_n, j_shift, j_scale = nxt
        in_specs += [vec, _ada_spec(D, j_shift, tpb), _ada_spec(D, j_scale, tpb)]
        args += [g_pre.reshape(1, D), ada_n, ada_n]
        out_specs.append(row)
        out_shape.append(jax.ShapeDtypeStruct((T, D), BF16))
    outs = pl.pallas_call(
        functools.partial(_post_kernel, with_next=nxt is not None),
        grid=(T // tm,),
        in_specs=in_specs,
        out_specs=out_specs,
        out_shape=out_shape,
        compiler_params=_params(("parallel",)),
        name="post_norm_residual",
    )(*args)
    return outs if nxt is not None else (outs[0], None)


def _mm_kernel(x_ref, w_ref, o_ref):
    o_ref[...] = jnp.dot(x_ref[...], w_ref[...], preferred_element_type=F32).astype(o_ref.dtype)


def _matmul(x, w, out_dtype, name):
    M, K = x.shape
    N = w.shape[1]
    tm, tn = _tile(M, 1024), _tile(N, 1024)
    return pl.pallas_call(
        _mm_kernel,
        grid=(M // tm, N // tn),
        in_specs=[pl.BlockSpec((tm, K), lambda i, j: (i, 0)),
                  pl.BlockSpec((K, tn), lambda i, j: (0, j))],
        out_specs=pl.BlockSpec((tm, tn), lambda i, j: (i, j)),
        out_shape=jax.ShapeDtypeStruct((M, N), out_dtype),
        compiler_params=_params(("parallel", "arbitrary")),
        name=name,
    )(x, w)


def _mm_nt_kernel(x_ref, wt_ref, o_ref):
    o_ref[...] = lax.dot_general(x_ref[...], wt_ref[...], (((1,), (1,)), ((), ())),
                                 preferred_element_type=F32).astype(o_ref.dtype)


def _matmul_nt(x, wt, layer, out_dtype, name):
    M, K = x.shape
    N = wt.shape[1]
    tm, tn = _tile(M, 1024), _tile(N, 1024)
    return pl.pallas_call(
        _mm_nt_kernel,
        grid=(M // tm, N // tn),
        in_specs=[pl.BlockSpec((tm, K), lambda i, j: (i, 0)),
                  pl.BlockSpec((None, tn, K), lambda i, j: (layer, j, 0))],
        out_specs=pl.BlockSpec((tm, tn), lambda i, j: (i, j)),
        out_shape=jax.ShapeDtypeStruct((M, N), out_dtype),
        compiler_params=_params(("parallel", "arbitrary")),
        name=name,
    )(x, wt)


def _sb_kernel(q_ref, k_ref, v_ref, *rest, blk, heads, n_cast):
    wf_refs, o_ref, wb_refs = rest[:n_cast], rest[n_cast], rest[n_cast + 1:2 * n_cast + 1]
    acc_ref, carry_ref = rest[2 * n_cast + 1:]
    for wf_ref, wb_ref in zip(wf_refs, wb_refs):
        wb_ref[...] = wf_ref[...].astype(BF16)
    qi = pl.program_id(2)
    row = lax.broadcasted_iota(jnp.int32, (blk, blk), 0)
    col = lax.broadcasted_iota(jnp.int32, (blk, blk), 1)
    neg_suffix = jnp.where(row >= col, -1.0, 0.0).astype(BF16)
    strict = col < row

    def lanes(g):
        return slice(g * HEAD_DIM, (g + 1) * HEAD_DIM)

    def scores(off, g):
        k = k_ref[pl.ds(off, blk), lanes(g)]
        z = lax.dot_general(q_ref[:, lanes(g)], k, (((1,), (1,)), ((), ())),
                            preferred_element_type=F32)
        softplus = jnp.maximum(z, 0.0) + jnp.log(1.0 + jnp.exp2(jnp.abs(z) * (-LOG2E)))
        return z, softplus

    def weighted_values(off, g, w):
        return jnp.dot(w.astype(BF16), v_ref[pl.ds(off, blk), lanes(g)], preferred_element_type=F32)

    def suffix_sums(sp):
        return jnp.dot(sp.astype(BF16), neg_suffix, preferred_element_type=F32)

    hs = range(heads)

    off = pl.multiple_of(qi * blk, blk)
    zs = [scores(off, g) for g in hs]
    sps = [jnp.where(strict, sp, 0.0) for _, sp in zs]
    css = [suffix_sums(sp) for sp in sps]
    ws = [jnp.where(strict, jnp.exp2((z + cs) * LOG2E), 0.0) for (z, _), cs in zip(zs, css)]
    for g in hs:
        acc_ref[g] = weighted_values(off, g, ws[g])
        carry_ref[g] = -jnp.sum(sps[g], axis=1, keepdims=True)

    def body(state):
        n, _ = state
        off = pl.multiple_of((qi - 1 - n) * blk, blk)
        zs = [scores(off, g) for g in hs]
        css = [suffix_sums(sp) for _, sp in zs]
        ws = [jnp.exp2((z + cs + carry_ref[g]) * LOG2E) for g, ((z, _), cs) in enumerate(zip(zs, css))]
        for g in hs:
            acc_ref[g] += weighted_values(off, g, ws[g])
            carry_ref[g] -= jnp.sum(zs[g][1], axis=1, keepdims=True)
        return n + 1, jnp.max(carry_ref[...]) > DEAD_LOG_MASS

    lax.while_loop(lambda st: jnp.logical_and(st[0] < qi, st[1]), body, (jnp.int32(0), True))
    for g in range(heads):
        o_ref[:, lanes(g)] = acc_ref[g].astype(o_ref.dtype)


def _slab_specs(w, layer, grid):
    _, K, N = w.shape
    steps = grid[0] * grid[1] * grid[2]
    rows = K // steps
    assert rows * steps == K and rows % 16 == 0
    index = lambda b, h, i: (layer, (b * grid[1] + h) * grid[2] + i, 0)
    spec = pl.BlockSpec((None, rows, N), index)
    return spec, pl.BlockSpec((rows, N), lambda b, h, i: index(b, h, i)[1:]), jax.ShapeDtypeStruct((K, N), BF16)


def _sb_attention(proj3d, w_casts, layer):
    B, S, _ = proj3d.shape
    blk = _tile(S, 256)
    G = SB_HEAD_GROUP
    gw = G * HEAD_DIM
    grid = (B, HEADS // G, S // blk)
    wf_specs, wb_specs, wb_shapes = zip(*[_slab_specs(w, layer, grid) for w in w_casts])
    q_spec = pl.BlockSpec((None, blk, gw), lambda b, h, i: (b, i, COL_Q // gw + h))
    k_spec = pl.BlockSpec((None, S, gw), lambda b, h, i: (b, 0, COL_K // gw + h))
    v_spec = pl.BlockSpec((None, S, gw), lambda b, h, i: (b, 0, COL_V // gw + h))
    outs = pl.pallas_call(
        functools.partial(_sb_kernel, blk=blk, heads=G, n_cast=len(w_casts)),
        grid=grid,
        in_specs=[q_spec, k_spec, v_spec, *wf_specs],
        out_specs=[pl.BlockSpec((None, blk, gw), lambda b, h, i: (b, i, h)), *wb_specs],
        out_shape=[jax.ShapeDtypeStruct((B, S, ATT_WIDTH), BF16), *wb_shapes],
        scratch_shapes=[pltpu.VMEM((G, blk, HEAD_DIM), F32), pltpu.VMEM((G, blk, 1), F32)],
        compiler_params=_params(("parallel", "parallel", "arbitrary")),
        name="stick_breaking_attention",
    )(proj3d, proj3d, proj3d, *w_casts)
    return outs[0], outs[1:]


def _rope_rotate(xs, cos, sin_signed):
    half = ROPE_DIM // 2
    swapped = pltpu.roll(xs, half, 1) + pltpu.roll(xs, LANE - half, 1)
    return xs * cos + swapped * sin_signed


def _mla_prep_kernel(qd_ref, kvd_ref, gq_ref, gkv_ref, wq_ref, wkv_ref, cos_ref, sin_ref,
                     qn_ref, qr_ref, kn_ref, v_ref, kr_ref, *, scale):
    cos = cos_ref[...]
    sin = sin_ref[...]
    c_q = _rms(qd_ref[...].astype(F32), gq_ref[...]).astype(BF16)
    yq = jnp.dot(c_q, wq_ref[...], preferred_element_type=F32)
    qn_ref[...] = (yq[:, :ATT_WIDTH] * scale).astype(BF16)
    for h in range(HEADS):
        lo = ATT_WIDTH + h * LANE
        qr_ref[:, h * LANE:(h + 1) * LANE] = (
            _rope_rotate(yq[:, lo:lo + LANE], cos, sin) * scale).astype(BF16)
    kvd = kvd_ref[...].astype(F32)
    c_kv = _rms(kvd[:, :KV_RANK], gkv_ref[...]).astype(BF16)
    ykv = jnp.dot(c_kv, wkv_ref[...], preferred_element_type=F32)
    kn_ref[...] = ykv[:, :ATT_WIDTH].astype(BF16)
    v_ref[...] = ykv[:, ATT_WIDTH:].astype(BF16)
    kr_ref[...] = _rope_rotate(kvd[:, KV_RANK:KV_RANK + LANE], cos, sin).astype(BF16)


def _mla_prep(proj, g_q, g_kv, wq, wkv, cos_t, sin_t):
    T = proj.shape[0]
    tm = _tile(T, 256)
    row = lambda w, c: pl.BlockSpec((tm, w), lambda i: (i, c))
    full = lambda a: pl.BlockSpec(a.shape, lambda i: (0, 0))
    wide = jax.ShapeDtypeStruct((T, ATT_WIDTH), BF16)
    return pl.pallas_call(
        functools.partial(_mla_prep_kernel, scale=LOG2E * (HEAD_DIM + ROPE_DIM) ** -0.5),
        grid=(T // tm,),
        in_specs=[row(1024, COL_QD // 1024), row(1024, COL_KVD // 1024),
                  pl.BlockSpec((1, Q_RANK), lambda i: (0, 0)),
                  pl.BlockSpec((1, KV_RANK), lambda i: (0, 0)),
                  full(wq), full(wkv), row(LANE, 0), row(LANE, 0)],
        out_specs=[row(ATT_WIDTH, 0)] * 4 + [row(LANE, 0)],
        out_shape=[wide, wide, wide, wide, jax.ShapeDtypeStruct((T, LANE), BF16)],
        compiler_params=_params(("parallel",)),
        name="latent_projections",
    )(proj, proj, g_q.reshape(1, Q_RANK), g_kv.reshape(1, KV_RANK), wq, wkv, cos_t, sin_t)


def _mla_kernel(qn_ref, qr_ref, kn_ref, kr_ref, v_ref, *rest, blk, heads, n_cast):
    wf_refs, o_ref, wb_refs = rest[:n_cast], rest[n_cast], rest[n_cast + 1:2 * n_cast + 1]
    acc_ref, m_ref = rest[2 * n_cast + 1:]
    for wf_ref, wb_ref in zip(wf_refs, wb_refs):
        wb_ref[...] = wf_ref[...].astype(BF16)
    qi = pl.program_id(2)
    acc_ref[...] = jnp.zeros_like(acc_ref)
    m_ref[...] = jnp.full_like(m_ref, NEG_BIG)

    def lanes(g):
        return slice(g * HEAD_DIM, (g + 1) * HEAD_DIM)

    def step(off, width, mask):
        kr = kr_ref[pl.ds(off, width), :]
        ones = jnp.ones((width, HEAD_DIM), BF16)
        ss = []
        for g in range(heads):
            q = jnp.concatenate([qn_ref[:, lanes(g)], qr_ref[:, lanes(g)]], axis=1)
            k = jnp.concatenate([kn_ref[pl.ds(off, width), lanes(g)], kr], axis=1)
            s = lax.dot_general(q, k, (((1,), (1,)), ((), ())), preferred_element_type=F32)
            ss.append(s if mask is None else jnp.where(mask, s, NEG_BIG))
        ps, alphas = [], []
        for g, s in enumerate(ss):
            m_old = m_ref[g]
            m_new = jnp.maximum(m_old, jnp.max(s, axis=1, keepdims=True))
            ps.append(jnp.exp2(s - jnp.tile(m_new, (1, width // LANE))).astype(BF16))
            alphas.append(jnp.exp2(m_old - m_new))
            m_ref[g] = m_new
        for g in range(heads):
            v = jnp.concatenate([v_ref[pl.ds(off, width), lanes(g)], ones], axis=1)
            acc_ref[g] = acc_ref[g] * jnp.tile(alphas[g], (1, 2)) + jnp.dot(
                ps[g], v, preferred_element_type=F32)

    def body(j, _):
        step(pl.multiple_of(j * (2 * blk), 2 * blk), 2 * blk, None)
        return 0

    lax.fori_loop(0, qi // 2, body, 0)

    @pl.when(qi % 2 == 1)
    def _():
        step(pl.multiple_of((qi - 1) * blk, blk), blk, None)

    row_chunk = lax.broadcasted_iota(jnp.int32, (blk, blk), 0) // CHUNK
    col_chunk = lax.broadcasted_iota(jnp.int32, (blk, blk), 1) // CHUNK
    step(pl.multiple_of(qi * blk, blk), blk, col_chunk <= row_chunk)
    for g in range(heads):
        acc = acc_ref[g]
        o_ref[:, lanes(g)] = (acc[:, :HEAD_DIM] / acc[:, HEAD_DIM:]).astype(o_ref.dtype)


def _mla_attention(qn, qr, kn, kr, v, B, S, w_casts, layer):
    blk = _tile(S, 256)
    G = MLA_HEAD_GROUP
    gw = G * HEAD_DIM
    grid = (B, HEADS // G, S // blk)
    wf_specs, wb_specs, wb_shapes = zip(*[_slab_specs(w, layer, grid) for w in w_casts])
    sh3 = lambda a: a.reshape(B, S, a.shape[-1])
    once = pl.Buffered(1)
    q_spec = pl.BlockSpec((None, blk, gw), lambda b, h, i: (b, i, h))
    kv_spec = pl.BlockSpec((None, S, gw), lambda b, h, i: (b, 0, h), pipeline_mode=once)
    kr_spec = pl.BlockSpec((None, S, LANE), lambda b, h, i: (b, 0, 0), pipeline_mode=once)
    outs = pl.pallas_call(
        functools.partial(_mla_kernel, blk=blk, heads=G, n_cast=len(w_casts)),
        grid=grid,
        in_specs=[q_spec, q_spec, kv_spec, kr_spec, kv_spec, *wf_specs],
        out_specs=[q_spec, *wb_specs],
        out_shape=[jax.ShapeDtypeStruct((B, S, ATT_WIDTH), BF16), *wb_shapes],
        scratch_shapes=[pltpu.VMEM((G, blk, 2 * HEAD_DIM), F32), pltpu.VMEM((G, blk, LANE), F32)],
        compiler_params=_params(("parallel", "parallel", "arbitrary")),
        name="latent_attention",
    )(sh3(qn), sh3(qr), sh3(kn), sh3(kr), sh3(v), *w_casts)
    return outs[0], outs[1:]


def _sigmoid(x):
    return 1.0 / (1.0 + jnp.exp(-x))


def _merge_kernel(osb_ref, omla_ref, wsb_ref, wmla_ref, gsb_ref, gmla_ref, o_ref):
    a = jnp.dot(osb_ref[...], wsb_ref[...], preferred_element_type=F32)
    b = jnp.dot(omla_ref[...], wmla_ref[...], preferred_element_type=F32)
    o_ref[...] = (_sigmoid(gsb_ref[...].astype(F32)) * a
                  + _sigmoid(gmla_ref[...].astype(F32)) * b).astype(o_ref.dtype)


def _merge(o_sb, o_mla, w_sb, w_mla, proj):
    T, W = o_sb.shape
    D = w_sb.shape[1]
    tm, tn = _tile(T, 512), _tile(D, 1024)
    gate0 = COL_GATE // tn
    act = pl.BlockSpec((tm, W), lambda i, j: (i, 0))
    wsp = pl.BlockSpec((W, tn), lambda i, j: (0, j))
    return pl.pallas_call(
        _merge_kernel,
        grid=(T // tm, D // tn),
        in_specs=[act, act, wsp, wsp,
                  pl.BlockSpec((tm, tn), lambda i, j: (i, gate0 + j)),
                  pl.BlockSpec((tm, tn), lambda i, j: (i, gate0 + D // tn + j))],
        out_specs=pl.BlockSpec((tm, tn), lambda i, j: (i, j)),
        out_shape=jax.ShapeDtypeStruct((T, D), BF16),
        compiler_params=_params(("parallel", "arbitrary")),
        name="gated_branch_merge",
    )(o_sb, o_mla, w_sb, w_mla, proj, proj)


def _mlp_kernel(h_ref, wu_ref, wd_ref, o_ref, acc_ref):
    f = pl.program_id(1)

    @pl.when(f == 0)
    def _():
        acc_ref[...] = jnp.zeros_like(acc_ref)

    u = jnp.dot(h_ref[...], wu_ref[...], preferred_element_type=F32)
    a = jnp.square(jnp.maximum(u, 0.0)).astype(BF16)
    acc_ref[...] += jnp.dot(a, wd_ref[...], preferred_element_type=F32)

    @pl.when(f == pl.num_programs(1) - 1)
    def _():
        o_ref[...] = acc_ref[...].astype(o_ref.dtype)


def _mlp(h, w_up, w_down):
    T, D = h.shape
    F = w_up.shape[1]
    tm, tf = _tile(T, 512), _tile(F, 512)
    return pl.pallas_call(
        _mlp_kernel,
        grid=(T // tm, F // tf),
        in_specs=[pl.BlockSpec((tm, D), lambda i, f: (i, 0)),
                  pl.BlockSpec((D, tf), lambda i, f: (0, f)),
                  pl.BlockSpec((tf, D), lambda i, f: (f, 0))],
        out_specs=pl.BlockSpec((tm, D), lambda i, f: (i, 0)),
        out_shape=jax.ShapeDtypeStruct((T, D), BF16),
        scratch_shapes=[pltpu.VMEM((tm, D), F32)],
        compiler_params=_params(("parallel", "arbitrary")),
        name="relu2_mlp",
    )(h, w_up, w_down)


W_IN_ROWS = 512
KVD_END = COL_KVD + KV_RANK + ROPE_DIM


def _w_in_kernel(w_ref, o_ref, *, q_scale):
    j = pl.program_id(1)
    tail_block = KVD_END // W_IN_ROWS

    @pl.when(j < ATT_WIDTH // W_IN_ROWS)
    def _():
        o_ref[...] = (w_ref[...] * q_scale).astype(BF16)

    @pl.when((j >= ATT_WIDTH // W_IN_ROWS) & (j != tail_block))
    def _():
        o_ref[...] = w_ref[...].astype(BF16)

    @pl.when(j == tail_block)
    def _():
        row = lax.broadcasted_iota(jnp.int32, w_ref.shape, 0)
        o_ref[...] = jnp.where(row < KVD_END - tail_block * W_IN_ROWS, w_ref[...], 0.0).astype(BF16)


def _prep_w_in(w_in):
    L, D, _ = w_in.shape
    wt = jnp.swapaxes(w_in, 1, 2)
    n_out = (COL_GATE + 2 * D) // W_IN_ROWS
    gate0 = COL_GATE // W_IN_ROWS
    assert KVD_END // W_IN_ROWS == gate0 - 1 and KVD_END % 8 == 0

    unit = 64
    assert W_IN_ROWS % unit == 0 and (COL_GATE - KVD_END) % unit == 0

    def src(l, j):
        shift = jnp.where(j < gate0, 0, (COL_GATE - KVD_END) // unit)
        return l, (j * (W_IN_ROWS // unit) - shift) * unit, 0

    return pl.pallas_call(
        functools.partial(_w_in_kernel, q_scale=HEAD_DIM ** -0.5),
        grid=(L, n_out),
        in_specs=[pl.BlockSpec((None, pl.Element(W_IN_ROWS), pl.Element(D)), src)],
        out_specs=pl.BlockSpec((None, W_IN_ROWS, D), lambda l, j: (l, j, 0)),
        out_shape=jax.ShapeDtypeStruct((L, n_out * W_IN_ROWS, D), BF16),
        compiler_params=_params(("parallel", "arbitrary")),
        name="input_weight_layout",
    )(wt)


def _prep_w_uq(w):
    w = w.reshape(Q_RANK, HEADS, HEAD_DIM + ROPE_DIM)
    nope = w[:, :, :HEAD_DIM].reshape(Q_RANK, ATT_WIDTH)
    rope = jnp.pad(w[:, :, HEAD_DIM:], ((0, 0), (0, 0), (0, LANE - ROPE_DIM))).reshape(Q_RANK, HEADS * LANE)
    return jnp.concatenate([nope, rope], axis=1).astype(BF16)


def _prep_w_ukv(w):
    w = w.reshape(KV_RANK, HEADS, 2 * HEAD_DIM)
    k = w[:, :, :HEAD_DIM].reshape(KV_RANK, ATT_WIDTH)
    v = w[:, :, HEAD_DIM:].reshape(KV_RANK, ATT_WIDTH)
    return jnp.concatenate([k, v], axis=1).astype(BF16)


def kernel(x, c, positions, w_ada, b_ada, g_pre_mix, g_post_mix, g_pre_mlp, g_post_mlp, w_in,
           g_q_lora, w_uq, g_kv_lora, w_ukv, w_o_sb, w_o_mla, w_out, w_up, w_down):
    B, S, D = x.shape
    L = w_ada.shape[0]
    T = B * S
    assert S % CHUNK == 0 and D % 512 == 0

    rows = 16
    c_pad = jnp.zeros((rows, D), F32).at[:B].set(c)
    ada = _ada(c_pad, w_ada, b_ada)[:, :B].reshape(L, B, 6, 1, D)
    cos_t, sin_t = _rope_tables(positions)

    w_in_b = _prep_w_in(w_in)

    x2d = x.reshape(T, D)
    h = _prenorm(x2d, g_pre_mix[0], ada[0], 0, 1, S)
    for l in range(L):
        proj = _matmul_nt(h, w_in_b, l, BF16, "input_projection")
        o_sb, (w_down_b,) = _sb_attention(proj.reshape(B, S, -1), (w_down,), l)
        o_sb = o_sb.reshape(T, ATT_WIDTH)
        qn, qr, kn, v, kr = _mla_prep(proj, g_q_lora[l], g_kv_lora[l], _prep_w_uq(w_uq[l]),
                                      _prep_w_ukv(w_ukv[l]), cos_t, sin_t)
        o_mla, (w_up_b, w_o_sb_b, w_o_mla_b, w_out_b) = _mla_attention(
            qn, qr, kn, kr, v, B, S, (w_up, w_o_sb, w_o_mla, w_out), l)
        o_mla = o_mla.reshape(T, ATT_WIDTH)
        merged = _merge(o_sb, o_mla, w_o_sb_b, w_o_mla_b, proj)
        y = _matmul(merged, w_out_b, BF16, "output_projection")
        x2d, h = _post(y, x2d, g_post_mix[l], ada[l], 2, S, nxt=(g_pre_mlp[l], ada[l], 3, 4))
        y = _mlp(h, w_up_b, w_down_b)
        nxt = (g_pre_mix[l + 1], ada[l + 1], 0, 1) if l + 1 < L else None
        x2d, h = _post(y, x2d, g_post_mlp[l], ada[l], 5, S, nxt=nxt)
    return x2d.reshape(B, S, D)
```

```python
import functools

import jax
import jax.numpy as jnp
from jax import lax
from jax.experimental import pallas as pl
from jax.experimental.pallas import tpu as pltpu

F32 = jnp.float32
BF16 = jnp.bfloat16

HEADS = 16
HEAD_DIM = 128
ATT_WIDTH = HEADS * HEAD_DIM
ROPE_DIM = 64
Q_RANK = 1024
KV_RANK = 512
CHUNK = 64
ROPE_THETA = 10000.0
EPS = 1e-6
NEG_BIG = -1e30
LOG2E = 1.4426950408889634
DEAD_LOG_MASS = -104.0
SB_HEAD_GROUP = 4
MLA_HEAD_GROUP = 8
MLA_BLOCK = 256
ONES_ROWS = 16

COL_Q, COL_K, COL_V = 0, ATT_WIDTH, 2 * ATT_WIDTH
COL_QD = 3 * ATT_WIDTH
COL_KVD = COL_QD + Q_RANK
COL_GATE = COL_KVD + 1024

V7X_VMEM_LIMIT_BYTES = 56 * 1024 * 1024
LANE = 128


def _params(sem):
    return pltpu.CompilerParams(dimension_semantics=sem, vmem_limit_bytes=V7X_VMEM_LIMIT_BYTES)


def _tile(n, pref):
    if n <= pref:
        return n
    t = (pref // LANE) * LANE
    while t >= LANE:
        if n % t == 0:
            return t
        t -= LANE
    raise ValueError(f"no tile for {n}")


def _silu(c):
    return c * (1.0 / (1.0 + jnp.exp(-c)))


def _ada_accumulate(c_ref, w_ref, b_ref, o_ref, first):
    @pl.when(first)
    def _():
        o_ref[...] = jnp.broadcast_to(b_ref[...], o_ref.shape)

    o_ref[...] += jnp.dot(_silu(c_ref[...]).astype(BF16), w_ref[...].astype(BF16),
                          preferred_element_type=F32)


def _ada_kernel(c_ref, w_ref, b_ref, o_ref):
    _ada_accumulate(c_ref, w_ref, b_ref, o_ref, pl.program_id(0) == 0)


def _ada(c_pad, w_ada, b_ada, layer):
    L, D, N = w_ada.shape
    R = c_pad.shape[0]
    tk = _tile(D, 128)
    return pl.pallas_call(
        _ada_kernel,
        grid=(D // tk,),
        in_specs=[pl.BlockSpec((R, tk), lambda k: (0, k)),
                  pl.BlockSpec((None, tk, N), lambda k: (layer, k, 0)),
                  pl.BlockSpec((None, 1, N), lambda k: (layer, 0, 0))],
        out_specs=pl.BlockSpec((R, N), lambda k: (0, 0)),
        out_shape=jax.ShapeDtypeStruct((R, N), F32),
        compiler_params=_params(("arbitrary",)),
        name="ada_proj",
    )(c_pad, w_ada, b_ada.reshape(L, 1, N))


def _rope_table_kernel(pos_ref, inv_ref, sgn_ref, cos_ref, sin_ref):
    ang = pos_ref[...].astype(F32) * inv_ref[...]
    cos_ref[...] = jnp.cos(ang)
    sin_ref[...] = jnp.sin(ang) * sgn_ref[...]


def _rope_tables(positions):
    T = positions.size
    half = ROPE_DIM // 2
    inv_freq = ROPE_THETA ** (-jnp.arange(half, dtype=F32) / half)
    zeros = jnp.zeros((LANE - ROPE_DIM,), F32)
    inv = jnp.concatenate([inv_freq, inv_freq, zeros]).reshape(1, LANE)
    sgn = jnp.concatenate([-jnp.ones((half,), F32), jnp.ones((half,), F32), zeros]).reshape(1, LANE)
    tm = _tile(T, 1024)
    tab = jax.ShapeDtypeStruct((T, LANE), F32)
    return pl.pallas_call(
        _rope_table_kernel,
        grid=(T // tm,),
        in_specs=[pl.BlockSpec((tm, 1), lambda i: (i, 0)),
                  pl.BlockSpec((1, LANE), lambda i: (0, 0)),
                  pl.BlockSpec((1, LANE), lambda i: (0, 0))],
        out_specs=[pl.BlockSpec((tm, LANE), lambda i: (i, 0))] * 2,
        out_shape=[tab, tab],
        compiler_params=_params(("parallel",)),
        name="rope_tables",
    )(positions.reshape(T, 1), inv, sgn)


def _rms(x, g):
    return x * lax.rsqrt(jnp.mean(x * x, axis=-1, keepdims=True) + EPS) * g


def _ada_spec(D, j, tiles_per_batch):
    return pl.BlockSpec((None, None, 1, D), lambda i: (i // tiles_per_batch, j, 0, 0))


def _prenorm_kernel(x_ref, g_ref, sh_ref, sc_ref, h_ref):
    h = _rms(x_ref[...], g_ref[...])
    h_ref[...] = (h * (1.0 + sc_ref[...]) + sh_ref[...]).astype(BF16)


def _prenorm(x2d, g, ada_l, j_shift, j_scale, S):
    T, D = x2d.shape
    tm = _tile(S, 512)
    tpb = S // tm
    row = pl.BlockSpec((tm, D), lambda i: (i, 0))
    vec = pl.BlockSpec((1, D), lambda i: (0, 0))
    return pl.pallas_call(
        _prenorm_kernel,
        grid=(T // tm,),
        in_specs=[row, vec, _ada_spec(D, j_shift, tpb), _ada_spec(D, j_scale, tpb)],
        out_specs=row,
        out_shape=jax.ShapeDtypeStruct((T, D), BF16),
        compiler_params=_params(("parallel",)),
        name="prenorm",
    )(x2d, g.reshape(1, D), ada_l, ada_l)


def _post_kernel(y_ref, x_ref, gpost_ref, gate_ref, *rest, with_next):
    y = y_ref[...].astype(F32)
    xn = x_ref[...] + gate_ref[...] * _rms(y, gpost_ref[...])
    if with_next:
        gpre_ref, sh_ref, sc_ref, xo_ref, h_ref = rest
        h = _rms(xn, gpre_ref[...])
        h_ref[...] = (h * (1.0 + sc_ref[...]) + sh_ref[...]).astype(BF16)
    else:
        (xo_ref,) = rest
    xo_ref[...] = xn


def _post(y, x2d, g_post, ada_l, j_gate, S, nxt=None):
    T, D = x2d.shape
    tm = _tile(S, 256)
    tpb = S // tm
    row = pl.BlockSpec((tm, D), lambda i: (i, 0))
    vec = pl.BlockSpec((1, D), lambda i: (0, 0))
    in_specs = [row, row, vec, _ada_spec(D, j_gate, tpb)]
    args = [y, x2d, g_post.reshape(1, D), ada_l]
    out_specs = [row]
    out_shape = [jax.ShapeDtypeStruct((T, D), F32)]
    if nxt is not None:
        g_pre, ada_n, j_shift, j_scale = nxt
        in_specs += [vec, _ada_spec(D, j_shift, tpb), _ada_spec(D, j_scale, tpb)]
        args += [g_pre.reshape(1, D), ada_n, ada_n]
        out_specs.append(row)
        out_shape.append(jax.ShapeDtypeStruct((T, D), BF16))
    outs = pl.pallas_call(
        functools.partial(_post_kernel, with_next=nxt is not None),
        grid=(T // tm,),
        in_specs=in_specs,
        out_specs=out_specs,
        out_shape=out_shape,
        compiler_params=_params(("parallel",)),
        name="post_norm_residual",
    )(*args)
    return outs if nxt is not None else (outs[0], None)


def _mm_kernel(x_ref, w_ref, o_ref):
    o_ref[...] = jnp.dot(x_ref[...], w_ref[...], preferred_element_type=F32).astype(o_ref.dtype)


def _matmul(x, w, out_dtype, name):
    M, K = x.shape
    N = w.shape[1]
    tm, tn = _tile(M, 1024), _tile(N, 1024)
    return pl.pallas_call(
        _mm_kernel,
        grid=(M // tm, N // tn),
        in_specs=[pl.BlockSpec((tm, K), lambda i, j: (i, 0)),
                  pl.BlockSpec((K, tn), lambda i, j: (0, j))],
        out_specs=pl.BlockSpec((tm, tn), lambda i, j: (i, j)),
        out_shape=jax.ShapeDtypeStruct((M, N), out_dtype),
        compiler_params=_params(("parallel", "arbitrary")),
        name=name,
    )(x, w)


def _mm_nt_kernel(x_ref, wt_ref, *rest, with_ada):
    o_ref = rest[-2] if with_ada else rest[-1]
    o_ref[...] = lax.dot_general(x_ref[...], wt_ref[...], (((1,), (1,)), ((), ())),
                                 preferred_element_type=F32).astype(o_ref.dtype)
    if with_ada:
        c_ref, wa_ref, ba_ref, _, ada_ref = rest
        first = jnp.logical_and(pl.program_id(0) == 0, pl.program_id(1) == 0)
        _ada_accumulate(c_ref, wa_ref, ba_ref, ada_ref, first)


def _matmul_nt(x, wt, out_dtype, name, ada=None):
    M, K = x.shape
    N = wt.shape[0]
    tm, tn = _tile(M, 1024), _tile(N, 1024)
    grid = (M // tm, N // tn)
    in_specs = [pl.BlockSpec((tm, K), lambda i, j: (i, 0)),
                pl.BlockSpec((tn, K), lambda i, j: (j, 0))]
    out_specs = [pl.BlockSpec((tm, tn), lambda i, j: (i, j))]
    out_shape = [jax.ShapeDtypeStruct((M, N), out_dtype)]
    args = [x, wt]
    if ada is not None:
        c_pad, w_ada, b_ada, layer = ada
        L, D, NA = w_ada.shape
        R = c_pad.shape[0]
        steps = grid[0] * grid[1]
        rk = D // steps
        assert rk * steps == D and rk % 8 == 0
        c_steps = c_pad.reshape(R, steps, rk).swapaxes(0, 1)
        step = lambda i, j: i * grid[1] + j
        in_specs += [pl.BlockSpec((None, R, rk), lambda i, j: (step(i, j), 0, 0)),
                     pl.BlockSpec((None, rk, NA), lambda i, j: (layer, step(i, j), 0)),
                     pl.BlockSpec((None, 1, NA), lambda i, j: (layer, 0, 0))]
        out_specs.append(pl.BlockSpec((R, NA), lambda i, j: (0, 0)))
        out_shape.append(jax.ShapeDtypeStruct((R, NA), F32))
        args += [c_steps, w_ada, b_ada.reshape(L, 1, NA)]
    outs = pl.pallas_call(
        functools.partial(_mm_nt_kernel, with_ada=ada is not None),
        grid=grid,
        in_specs=in_specs,
        out_specs=out_specs,
        out_shape=out_shape,
        compiler_params=_params(("arbitrary", "arbitrary") if ada is not None else ("parallel", "arbitrary")),
        name=name,
    )(*args)
    return outs if ada is not None else (outs[0], None)


def _ada_can_ride(T, D):
    steps = (T // _tile(T, 1024)) * ((COL_GATE + 2 * D) // 1024)
    return D % steps == 0 and (D // steps) % 8 == 0


def _sb_kernel(q_ref, k_ref, v_ref, *rest, blk, heads, n_cast, relayout):
    n_in = n_cast + relayout
    wf_refs, o_ref, wb_refs = rest[:n_in], rest[n_in], rest[n_in + 1:2 * n_in + 1]
    acc_ref, carry_ref = rest[2 * n_in + 1:]
    for wf_ref, wb_ref in zip(wf_refs[:n_cast], wb_refs):
        wb_ref[...] = wf_ref[...].astype(BF16)
    if relayout:
        step = (pl.program_id(0) * pl.num_programs(1) + pl.program_id(1)) * pl.num_programs(2) + pl.program_id(2)
        _w_in_block(wf_refs[-1], wb_refs[-1], step * wb_refs[-1].shape[0])
    qi = pl.program_id(2)
    row = lax.broadcasted_iota(jnp.int32, (blk, blk), 0)
    col = lax.broadcasted_iota(jnp.int32, (blk, blk), 1)
    neg_suffix = jnp.where(row >= col, -1.0, 0.0).astype(BF16)
    strict = col < row

    def lanes(g):
        return slice(g * HEAD_DIM, (g + 1) * HEAD_DIM)

    def scores(off, g):
        k = k_ref[pl.ds(off, blk), lanes(g)]
        z = lax.dot_general(q_ref[:, lanes(g)], k, (((1,), (1,)), ((), ())),
                            preferred_element_type=F32)
        softplus = jnp.maximum(z, 0.0) + jnp.log(1.0 + jnp.exp2(jnp.abs(z) * (-LOG2E)))
        return z, softplus

    def weighted_values(off, g, w):
        return jnp.dot(w.astype(BF16), v_ref[pl.ds(off, blk), lanes(g)], preferred_element_type=F32)

    def suffix_sums(sp):
        return jnp.dot(sp.astype(BF16), neg_suffix, preferred_element_type=F32)

    hs = range(heads)

    off = pl.multiple_of(qi * blk, blk)
    zs = [scores(off, g) for g in hs]
    sps = [jnp.where(strict, sp, 0.0) for _, sp in zs]
    css = [suffix_sums(sp) for sp in sps]
    ws = [jnp.where(strict, jnp.exp2((z + cs) * LOG2E), 0.0) for (z, _), cs in zip(zs, css)]
    for g in hs:
        acc_ref[g] = weighted_values(off, g, ws[g])
        carry_ref[g] = -jnp.sum(sps[g], axis=1, keepdims=True)

    def body(state):
        n, _ = state
        off = pl.multiple_of((qi - 1 - n) * blk, blk)
        zs = [scores(off, g) for g in hs]
        css = [suffix_sums(sp) for _, sp in zs]
        ws = [jnp.exp2((z + cs + carry_ref[g]) * LOG2E) for g, ((z, _), cs) in enumerate(zip(zs, css))]
        for g in hs:
            acc_ref[g] += weighted_values(off, g, ws[g])
            carry_ref[g] -= jnp.sum(zs[g][1], axis=1, keepdims=True)
        return n + 1, jnp.max(carry_ref[...]) > DEAD_LOG_MASS

    lax.while_loop(lambda st: jnp.logical_and(st[0] < qi, st[1]), body, (jnp.int32(0), True))
    for g in range(heads):
        o_ref[:, lanes(g)] = acc_ref[g].astype(o_ref.dtype)


def _slab_specs(w, layer, grid):
    _, K, N = w.shape
    steps = grid[0] * grid[1] * grid[2]
    rows = K // steps
    assert rows * steps == K and rows % 16 == 0
    index = lambda b, h, i: (layer, (b * grid[1] + h) * grid[2] + i, 0)
    spec = pl.BlockSpec((None, rows, N), index)
    return spec, pl.BlockSpec((rows, N), lambda b, h, i: index(b, h, i)[1:]), jax.ShapeDtypeStruct((K, N), BF16)


def _sb_attention(proj3d, w_casts, layer, w_in_next=None):
    B, S, _ = proj3d.shape
    blk = _tile(S, 256)
    G = SB_HEAD_GROUP
    gw = G * HEAD_DIM
    grid = (B, HEADS // G, S // blk)
    steps = grid[0] * grid[1] * grid[2]
    step = lambda b, h, i: (b * grid[1] + h) * grid[2] + i
    wf_specs, wb_specs, wb_shapes = map(list, zip(*[_slab_specs(w, layer, grid) for w in w_casts]))
    args = [proj3d, proj3d, proj3d, *w_casts]
    if w_in_next is not None:
        wt, nxt = w_in_next
        src_spec, rows, n_pad = _w_in_window(wt, steps, lambda b, h, i: (nxt, step(b, h, i)))
        wf_specs.append(src_spec)
        wb_specs.append(pl.BlockSpec((rows, wt.shape[2]), lambda b, h, i: (step(b, h, i), 0)))
        wb_shapes.append(jax.ShapeDtypeStruct((n_pad, wt.shape[2]), BF16))
        args.append(wt)
    q_spec = pl.BlockSpec((None, blk, gw), lambda b, h, i: (b, i, COL_Q // gw + h))
    k_spec = pl.BlockSpec((None, S, gw), lambda b, h, i: (b, 0, COL_K // gw + h))
    v_spec = pl.BlockSpec((None, S, gw), lambda b, h, i: (b, 0, COL_V // gw + h))
    outs = pl.pallas_call(
        functools.partial(_sb_kernel, blk=blk, heads=G, n_cast=len(w_casts), relayout=w_in_next is not None),
        grid=grid,
        in_specs=[q_spec, k_spec, v_spec, *wf_specs],
        out_specs=[pl.BlockSpec((None, blk, gw), lambda b, h, i: (b, i, h)), *wb_specs],
        out_shape=[jax.ShapeDtypeStruct((B, S, ATT_WIDTH), BF16), *wb_shapes],
        scratch_shapes=[pltpu.VMEM((G, blk, HEAD_DIM), F32), pltpu.VMEM((G, blk, 1), F32)],
        compiler_params=_params(("parallel", "parallel", "arbitrary")),
        name="stick_breaking_attention",
    )(*args)
    return outs[0], outs[1:]


def _rope_rotate(xs, cos, sin_signed):
    half = ROPE_DIM // 2
    swapped = pltpu.roll(xs, half, 1) + pltpu.roll(xs, LANE - half, 1)
    return xs * cos + swapped * sin_signed


def _mla_prep_kernel(qd_ref, kvd_ref, gq_ref, gkv_ref, wq_ref, wkv_ref, cos_ref, sin_ref,
                     qn_ref, qr_ref, kn_ref, vt_ref, kr_ref, *, scale):
    cos = cos_ref[...]
    sin = sin_ref[...]
    c_q = _rms(qd_ref[...].astype(F32), gq_ref[...]).astype(BF16)
    yq = jnp.dot(c_q, wq_ref[...], preferred_element_type=F32)
    qn_ref[...] = (yq[:, :ATT_WIDTH] * scale).astype(BF16)
    for h in range(HEADS):
        lo = ATT_WIDTH + h * LANE
        qr_ref[:, h * LANE:(h + 1) * LANE] = (
            _rope_rotate(yq[:, lo:lo + LANE], cos, sin) * scale).astype(BF16)
    kvd = kvd_ref[...].astype(F32)
    c_kv = _rms(kvd[:, :KV_RANK], gkv_ref[...]).astype(BF16)
    ykv = jnp.dot(c_kv, wkv_ref[...], preferred_element_type=F32)
    kn_ref[...] = ykv[:, :ATT_WIDTH].astype(BF16)
    vt_ref[...] = ykv[:, ATT_WIDTH:].T.astype(BF16)
    kr_ref[...] = _rope_rotate(kvd[:, KV_RANK:KV_RANK + LANE], cos, sin).astype(BF16)


def _mla_prep(proj, g_q, g_kv, wq, wkv, cos_t, sin_t, B, S):
    T = proj.shape[0]
    tm = _tile(S, MLA_BLOCK)
    tpb = S // tm
    row = lambda w, c: pl.BlockSpec((tm, w), lambda i: (i, c))
    full = lambda a: pl.BlockSpec(a.shape, lambda i: (0, 0))
    wide = jax.ShapeDtypeStruct((T, ATT_WIDTH), BF16)
    return pl.pallas_call(
        functools.partial(_mla_prep_kernel, scale=LOG2E * (HEAD_DIM + ROPE_DIM) ** -0.5),
        grid=(T // tm,),
        in_specs=[row(1024, COL_QD // 1024), row(1024, COL_KVD // 1024),
                  pl.BlockSpec((1, Q_RANK), lambda i: (0, 0)),
                  pl.BlockSpec((1, KV_RANK), lambda i: (0, 0)),
                  full(wq), full(wkv), row(LANE, 0), row(LANE, 0)],
        out_specs=[row(ATT_WIDTH, 0)] * 3
                  + [pl.BlockSpec((None, None, ATT_WIDTH, tm), lambda i: (i // tpb, i % tpb, 0, 0)),
                     row(LANE, 0)],
        out_shape=[wide, wide, wide, jax.ShapeDtypeStruct((B, tpb, ATT_WIDTH, tm), BF16),
                   jax.ShapeDtypeStruct((T, LANE), BF16)],
        compiler_params=_params(("parallel",)),
        name="latent_projections",
    )(proj, proj, g_q.reshape(1, Q_RANK), g_kv.reshape(1, KV_RANK), wq, wkv, cos_t, sin_t)


def _mla_kernel(qn_ref, qr_ref, kn_ref, kr_ref, vt_ref, *rest, blk, heads, n_cast):
    wf_refs, o_ref, wb_refs = rest[:n_cast], rest[n_cast], rest[n_cast + 1:2 * n_cast + 1]
    acc_ref, m_ref = rest[2 * n_cast + 1:]
    for wf_ref, wb_ref in zip(wf_refs, wb_refs):
        wb_ref[...] = wf_ref[...].astype(BF16)
    qi = pl.program_id(2)
    acc_ref[...] = jnp.zeros_like(acc_ref)
    m_ref[...] = jnp.full_like(m_ref, NEG_BIG)

    def lanes(g):
        return slice(g * HEAD_DIM, (g + 1) * HEAD_DIM)

    def step(kb, n_kb, mask):
        width = n_kb * blk
        off = pl.multiple_of(kb * blk, blk)
        kr = kr_ref[pl.ds(off, width), :]
        ones = jnp.ones((ONES_ROWS, width), BF16)
        ss = []
        for g in range(heads):
            q = jnp.concatenate([qn_ref[:, lanes(g)], qr_ref[:, lanes(g)]], axis=1)
            k = jnp.concatenate([kn_ref[pl.ds(off, width), lanes(g)], kr], axis=1)
            s = lax.dot_general(k, q, (((1,), (1,)), ((), ())), preferred_element_type=F32)
            ss.append(s if mask is None else jnp.where(mask, s, NEG_BIG))
        ps, alphas = [], []
        for g, s in enumerate(ss):
            m_old = m_ref[g][0:1, :]
            m_new = jnp.maximum(m_old, jnp.max(s, axis=0, keepdims=True))
            ps.append(jnp.exp2(s - m_new).astype(BF16))
            alphas.append(jnp.exp2(m_old - m_new))
            m_ref[g] = jnp.broadcast_to(m_new, m_ref.shape[1:])
        for g in range(heads):
            vt = jnp.concatenate([vt_ref[kb + n, lanes(g), :] for n in range(n_kb)] , axis=1)
            vt = jnp.concatenate([vt, ones], axis=0)
            acc_ref[g] = acc_ref[g] * alphas[g] + jnp.dot(vt, ps[g], preferred_element_type=F32)

    def body(j, _):
        step(2 * j, 2, None)
        return 0

    lax.fori_loop(0, qi // 2, body, 0)

    @pl.when(qi % 2 == 1)
    def _():
        step(qi - 1, 1, None)

    key_chunk = lax.broadcasted_iota(jnp.int32, (blk, blk), 0) // CHUNK
    query_chunk = lax.broadcasted_iota(jnp.int32, (blk, blk), 1) // CHUNK
    step(qi, 1, key_chunk <= query_chunk)
    for g in range(heads):
        acc = acc_ref[g]
        out_t = acc[:HEAD_DIM, :] / acc[HEAD_DIM:HEAD_DIM + 1, :]
        o_ref[:, lanes(g)] = out_t.T.astype(o_ref.dtype)


def _mla_attention(qn, qr, kn, kr, vt, B, S, w_casts, layer):
    blk = _tile(S, MLA_BLOCK)
    G = MLA_HEAD_GROUP
    gw = G * HEAD_DIM
    grid = (B, HEADS // G, S // blk)
    wf_specs, wb_specs, wb_shapes = zip(*[_slab_specs(w, layer, grid) for w in w_casts])
    sh3 = lambda a: a.reshape(B, S, a.shape[-1])
    once = pl.Buffered(1)
    q_spec = pl.BlockSpec((None, blk, gw), lambda b, h, i: (b, i, h))
    k_spec = pl.BlockSpec((None, S, gw), lambda b, h, i: (b, 0, h), pipeline_mode=once)
    kr_spec = pl.BlockSpec((None, S, LANE), lambda b, h, i: (b, 0, 0), pipeline_mode=once)
    vt_spec = pl.BlockSpec((None, S // blk, gw, blk), lambda b, h, i: (b, 0, h, 0), pipeline_mode=once)
    outs = pl.pallas_call(
        functools.partial(_mla_kernel, blk=blk, heads=G, n_cast=len(w_casts)),
        grid=grid,
        in_specs=[q_spec, q_spec, k_spec, kr_spec, vt_spec, *wf_specs],
        out_specs=[q_spec, *wb_specs],
        out_shape=[jax.ShapeDtypeStruct((B, S, ATT_WIDTH), BF16), *wb_shapes],
        scratch_shapes=[pltpu.VMEM((G, HEAD_DIM + ONES_ROWS, blk), F32), pltpu.VMEM((G, 8, blk), F32)],
        compiler_params=_params(("parallel", "parallel", "arbitrary")),
        name="latent_attention",
    )(sh3(qn), sh3(qr), sh3(kn), sh3(kr), vt, *w_casts)
    return outs[0], outs[1:]


def _sigmoid(x):
    return 1.0 / (1.0 + jnp.exp(-x))


def _merge_kernel(osb_ref, omla_ref, wsb_ref, wmla_ref, gsb_ref, gmla_ref, o_ref):
    a = jnp.dot(osb_ref[...], wsb_ref[...], preferred_element_type=F32)
    b = jnp.dot(omla_ref[...], wmla_ref[...], preferred_element_type=F32)
    o_ref[...] = (_sigmoid(gsb_ref[...].astype(F32)) * a
                  + _sigmoid(gmla_ref[...].astype(F32)) * b).astype(o_ref.dtype)


def _merge(o_sb, o_mla, w_sb, w_mla, proj):
    T, W = o_sb.shape
    D = w_sb.shape[1]
    tm, tn = _tile(T, 512), _tile(D, 1024)
    gate0 = COL_GATE // tn
    act = pl.BlockSpec((tm, W), lambda i, j: (i, 0))
    wsp = pl.BlockSpec((W, tn), lambda i, j: (0, j))
    return pl.pallas_call(
        _merge_kernel,
        grid=(T // tm, D // tn),
        in_specs=[act, act, wsp, wsp,
                  pl.BlockSpec((tm, tn), lambda i, j: (i, gate0 + j)),
                  pl.BlockSpec((tm, tn), lambda i, j: (i, gate0 + D // tn + j))],
        out_specs=pl.BlockSpec((tm, tn), lambda i, j: (i, j)),
        out_shape=jax.ShapeDtypeStruct((T, D), BF16),
        compiler_params=_params(("parallel", "arbitrary")),
        name="gated_branch_merge",
    )(o_sb, o_mla, w_sb, w_mla, proj, proj)


def _mlp_kernel(h_ref, wu_ref, wd_ref, o_ref, acc_ref):
    f = pl.program_id(1)

    @pl.when(f == 0)
    def _():
        acc_ref[...] = jnp.zeros_like(acc_ref)

    u = jnp.dot(h_ref[...], wu_ref[...], preferred_element_type=F32)
    a = jnp.square(jnp.maximum(u, 0.0)).astype(BF16)
    acc_ref[...] += jnp.dot(a, wd_ref[...], preferred_element_type=F32)

    @pl.when(f == pl.num_programs(1) - 1)
    def _():
        o_ref[...] = acc_ref[...].astype(o_ref.dtype)


def _mlp(h, w_up, w_down):
    T, D = h.shape
    F = w_up.shape[1]
    tm, tf = _tile(T, 512), _tile(F, 1024)
    once = pl.Buffered(1)
    return pl.pallas_call(
        _mlp_kernel,
        grid=(T // tm, F // tf),
        in_specs=[pl.BlockSpec((tm, D), lambda i, f: (i, 0), pipeline_mode=once),
                  pl.BlockSpec((D, tf), lambda i, f: (0, f)),
                  pl.BlockSpec((tf, D), lambda i, f: (f, 0))],
        out_specs=pl.BlockSpec((tm, D), lambda i, f: (i, 0), pipeline_mode=once),
        out_shape=jax.ShapeDtypeStruct((T, D), BF16),
        scratch_shapes=[pltpu.VMEM((tm, D), F32)],
        compiler_params=_params(("parallel", "arbitrary")),
        name="relu2_mlp",
    )(h, w_up, w_down)


KVD_END = COL_KVD + KV_RANK + ROPE_DIM
GATE_SHIFT = COL_GATE - KVD_END


def _w_in_block(w_ref, o_ref, row0):
    r = row0 + lax.broadcasted_iota(jnp.int32, o_ref.shape, 0)
    w = w_ref[...]
    w = jnp.where(r < ATT_WIDTH, w * (HEAD_DIM ** -0.5), w)
    keep = jnp.where(r < KVD_END, 1, 0) + jnp.where(r >= COL_GATE, 1, 0)
    o_ref[...] = jnp.where(keep > 0, w, 0.0).astype(BF16)


def _w_in_window(wt, n_blocks, where):
    _, n_src, D = wt.shape
    n_pad = n_src + GATE_SHIFT
    rows = n_pad // n_blocks
    assert rows * n_blocks == n_pad and rows % 16 == 0
    unit = 8
    while rows % (2 * unit) == 0 and GATE_SHIFT % (2 * unit) == 0:
        unit *= 2
    assert all(not (j * rows < KVD_END and (j + 1) * rows > COL_GATE) for j in range(n_blocks))

    def src(*ids):
        layer, j = where(*ids)
        shift = jnp.where(j * rows < KVD_END, 0, GATE_SHIFT // unit)
        return layer, (j * (rows // unit) - shift) * unit, 0

    return pl.BlockSpec((None, pl.Element(rows), pl.Element(D)), src), rows, n_pad


def _w_in_kernel(w_ref, o_ref):
    _w_in_block(w_ref, o_ref, pl.program_id(0) * o_ref.shape[0])


def _prep_w_in(wt, layer):
    n_blocks = (wt.shape[1] + GATE_SHIFT) // 512
    src_spec, rows, n_pad = _w_in_window(wt, n_blocks, lambda j: (layer, j))
    return pl.pallas_call(
        _w_in_kernel,
        grid=(n_blocks,),
        in_specs=[src_spec],
        out_specs=pl.BlockSpec((rows, wt.shape[2]), lambda j: (j, 0)),
        out_shape=jax.ShapeDtypeStruct((n_pad, wt.shape[2]), BF16),
        compiler_params=_params(("parallel",)),
        name="input_weight_layout",
    )(wt)


def _prep_w_uq(w):
    w = w.reshape(Q_RANK, HEADS, HEAD_DIM + ROPE_DIM)
    nope = w[:, :, :HEAD_DIM].reshape(Q_RANK, ATT_WIDTH)
    rope = jnp.pad(w[:, :, HEAD_DIM:], ((0, 0), (0, 0), (0, LANE - ROPE_DIM))).reshape(Q_RANK, HEADS * LANE)
    return jnp.concatenate([nope, rope], axis=1).astype(BF16)


def _prep_w_ukv(w):
    w = w.reshape(KV_RANK, HEADS, 2 * HEAD_DIM)
    k = w[:, :, :HEAD_DIM].reshape(KV_RANK, ATT_WIDTH)
    v = w[:, :, HEAD_DIM:].reshape(KV_RANK, ATT_WIDTH)
    return jnp.concatenate([k, v], axis=1).astype(BF16)


def kernel(x, c, positions, w_ada, b_ada, g_pre_mix, g_post_mix, g_pre_mlp, g_post_mlp, w_in,
           g_q_lora, w_uq, g_kv_lora, w_ukv, w_o_sb, w_o_mla, w_out, w_up, w_down):
    B, S, D = x.shape
    L = w_ada.shape[0]
    T = B * S
    assert S % CHUNK == 0 and D % 512 == 0

    rows = 16
    c_pad = jnp.zeros((rows, D), F32).at[:B].set(c)
    ride_ada = _ada_can_ride(T, D)
    six = lambda a: a[:B].reshape(B, 6, 1, D)
    ada = [six(_ada(c_pad, w_ada, b_ada, l)) if (l == 0 or not ride_ada) else None for l in range(L)]
    cos_t, sin_t = _rope_tables(positions)

    w_in_t = jnp.swapaxes(w_in, 1, 2)
    w_in_b = _prep_w_in(w_in_t, 0)

    x2d = x.reshape(T, D)
    h = _prenorm(x2d, g_pre_mix[0], ada[0], 0, 1, S)
    for l in range(L):
        more = l + 1 < L
        proj, ada_next = _matmul_nt(h, w_in_b, BF16, "input_projection",
                                    ada=(c_pad, w_ada, b_ada, l + 1) if more and ride_ada else None)
        if ada_next is not None:
            ada[l + 1] = six(ada_next)
        o_sb, sb_extra = _sb_attention(proj.reshape(B, S, -1), (w_down,), l,
                                       w_in_next=(w_in_t, l + 1) if more else None)
        w_down_b = sb_extra[0]
        if more:
            w_in_b = sb_extra[1]
        o_sb = o_sb.reshape(T, ATT_WIDTH)
        qn, qr, kn, vt, kr = _mla_prep(proj, g_q_lora[l], g_kv_lora[l], _prep_w_uq(w_uq[l]),
                                       _prep_w_ukv(w_ukv[l]), cos_t, sin_t, B, S)
        o_mla, (w_up_b, w_o_sb_b, w_o_mla_b, w_out_b) = _mla_attention(
            qn, qr, kn, kr, vt, B, S, (w_up, w_o_sb, w_o_mla, w_out), l)
        o_mla = o_mla.reshape(T, ATT_WIDTH)
        merged = _merge(o_sb, o_mla, w_o_sb_b, w_o_mla_b, proj)
        y = _matmul(merged, w_out_b, BF16, "output_projection")
        x2d, h = _post(y, x2d, g_post_mix[l], ada[l], 2, S, nxt=(g_pre_mlp[l], ada[l], 3, 4))
        y = _mlp(h, w_up_b, w_down_b)
        nxt = (g_pre_mix[l + 1], ada[l + 1], 0, 1) if more else None
        x2d, h = _post(y, x2d, g_post_mlp[l], ada[l], 5, S, nxt=nxt)
    return x2d.reshape(B, S, D)
```

```python
import functools

import jax
import jax.numpy as jnp
from jax import lax
from jax.experimental import pallas as pl
from jax.experimental.pallas import tpu as pltpu

F32 = jnp.float32
BF16 = jnp.bfloat16

HEADS = 16
HEAD_DIM = 128
ATT_WIDTH = HEADS * HEAD_DIM
ROPE_DIM = 64
Q_RANK = 1024
KV_RANK = 512
CHUNK = 64
ROPE_THETA = 10000.0
EPS = 1e-6
NEG_BIG = -1e30
LOG2E = 1.4426950408889634
DEAD_LOG_MASS = -104.0
SB_HEAD_GROUP = 4
MLA_HEAD_GROUP = 8
MLA_BLOCK = 256
ONES_ROWS = 16

COL_Q, COL_K, COL_V = 0, ATT_WIDTH, 2 * ATT_WIDTH
COL_QD = 3 * ATT_WIDTH
COL_KVD = COL_QD + Q_RANK
COL_GATE = COL_KVD + 1024

V7X_VMEM_LIMIT_BYTES = 56 * 1024 * 1024
LANE = 128


def _params(sem):
    return pltpu.CompilerParams(dimension_semantics=sem, vmem_limit_bytes=V7X_VMEM_LIMIT_BYTES)


def _tile(n, pref):
    if n <= pref:
        return n
    t = (pref // LANE) * LANE
    while t >= LANE:
        if n % t == 0:
            return t
        t -= LANE
    raise ValueError(f"no tile for {n}")


def _silu(c):
    return c * (1.0 / (1.0 + jnp.exp(-c)))


def _ada_accumulate(c_ref, w_ref, b_ref, o_ref, first):
    @pl.when(first)
    def _():
        o_ref[...] = jnp.broadcast_to(b_ref[...], o_ref.shape)

    o_ref[...] += jnp.dot(_silu(c_ref[...]).astype(BF16), w_ref[...].astype(BF16),
                          preferred_element_type=F32)


def _ada_kernel(c_ref, w_ref, b_ref, o_ref):
    _ada_accumulate(c_ref, w_ref, b_ref, o_ref, pl.program_id(0) == 0)


def _ada(c_pad, w_ada, b_ada, layer):
    L, D, N = w_ada.shape
    R = c_pad.shape[0]
    tk = _tile(D, 128)
    return pl.pallas_call(
        _ada_kernel,
        grid=(D // tk,),
        in_specs=[pl.BlockSpec((R, tk), lambda k: (0, k)),
                  pl.BlockSpec((None, tk, N), lambda k: (layer, k, 0)),
                  pl.BlockSpec((None, 1, N), lambda k: (layer, 0, 0))],
        out_specs=pl.BlockSpec((R, N), lambda k: (0, 0)),
        out_shape=jax.ShapeDtypeStruct((R, N), F32),
        compiler_params=_params(("arbitrary",)),
        name="ada_proj",
    )(c_pad, w_ada, b_ada.reshape(L, 1, N))


def _rope_table_kernel(pos_ref, inv_ref, sgn_ref, cos_ref, sin_ref):
    ang = pos_ref[...].astype(F32) * inv_ref[...]
    cos_ref[...] = jnp.cos(ang)
    sin_ref[...] = jnp.sin(ang) * sgn_ref[...]


def _rope_tables(positions):
    T = positions.size
    half = ROPE_DIM // 2
    inv_freq = ROPE_THETA ** (-jnp.arange(half, dtype=F32) / half)
    zeros = jnp.zeros((LANE - ROPE_DIM,), F32)
    inv = jnp.concatenate([inv_freq, inv_freq, zeros]).reshape(1, LANE)
    sgn = jnp.concatenate([-jnp.ones((half,), F32), jnp.ones((half,), F32), zeros]).reshape(1, LANE)
    tm = _tile(T, 1024)
    tab = jax.ShapeDtypeStruct((T, LANE), F32)
    return pl.pallas_call(
        _rope_table_kernel,
        grid=(T // tm,),
        in_specs=[pl.BlockSpec((tm, 1), lambda i: (i, 0)),
                  pl.BlockSpec((1, LANE), lambda i: (0, 0)),
                  pl.BlockSpec((1, LANE), lambda i: (0, 0))],
        out_specs=[pl.BlockSpec((tm, LANE), lambda i: (i, 0))] * 2,
        out_shape=[tab, tab],
        compiler_params=_params(("parallel",)),
        name="rope_tables",
    )(positions.reshape(T, 1), inv, sgn)


def _rms(x, g):
    return x * lax.rsqrt(jnp.mean(x * x, axis=-1, keepdims=True) + EPS) * g


def _ada_spec(D, j, tiles_per_batch):
    return pl.BlockSpec((None, None, 1, D), lambda i: (i // tiles_per_batch, j, 0, 0))


def _prenorm_kernel(x_ref, g_ref, sh_ref, sc_ref, h_ref):
    h = _rms(x_ref[...], g_ref[...])
    h_ref[...] = (h * (1.0 + sc_ref[...]) + sh_ref[...]).astype(BF16)


def _prenorm(x2d, g, ada_l, j_shift, j_scale, S):
    T, D = x2d.shape
    tm = _tile(S, 512)
    tpb = S // tm
    row = pl.BlockSpec((tm, D), lambda i: (i, 0))
    vec = pl.BlockSpec((1, D), lambda i: (0, 0))
    return pl.pallas_call(
        _prenorm_kernel,
        grid=(T // tm,),
        in_specs=[row, vec, _ada_spec(D, j_shift, tpb), _ada_spec(D, j_scale, tpb)],
        out_specs=row,
        out_shape=jax.ShapeDtypeStruct((T, D), BF16),
        compiler_params=_params(("parallel",)),
        name="prenorm",
    )(x2d, g.reshape(1, D), ada_l, ada_l)


def _post_kernel(y_ref, x_ref, gpost_ref, gate_ref, *rest, with_next):
    y = y_ref[...].astype(F32)
    xn = x_ref[...] + gate_ref[...] * _rms(y, gpost_ref[...])
    if with_next:
        gpre_ref, sh_ref, sc_ref, xo_ref, h_ref = rest
        h = _rms(xn, gpre_ref[...])
        h_ref[...] = (h * (1.0 + sc_ref[...]) + sh_ref[...]).astype(BF16)
    else:
        (xo_ref,) = rest
    xo_ref[...] = xn


def _post(y, x2d, g_post, ada_l, j_gate, S, nxt=None):
    T, D = x2d.shape
    tm = _tile(S, 256)
    tpb = S // tm
    row = pl.BlockSpec((tm, D), lambda i: (i, 0))
    vec = pl.BlockSpec((1, D), lambda i: (0, 0))
    in_specs = [row, row, vec, _ada_spec(D, j_gate, tpb)]
    args = [y, x2d, g_post.reshape(1, D), ada_l]
    out_specs = [row]
    out_shape = [jax.ShapeDtypeStruct((T, D), F32)]
    if nxt is not None:
        g_pre, ada_n, j_shift, j_scale = nxt
        in_specs += [vec, _ada_spec(D, j_shift, tpb), _ada_spec(D, j_scale, tpb)]
        args += [g_pre.reshape(1, D), ada_n, ada_n]
        out_specs.append(row)
        out_shape.append(jax.ShapeDtypeStruct((T, D), BF16))
    outs = pl.pallas_call(
        functools.partial(_post_kernel, with_next=nxt is not None),
        grid=(T // tm,),
        in_specs=in_specs,
        out_specs=out_specs,
        out_shape=out_shape,
        compiler_params=_params(("parallel",)),
        name="post_norm_residual",
    )(*args)
    return outs if nxt is not None else (outs[0], None)


def _mm_kernel(x_ref, w_ref, o_ref):
    o_ref[...] = jnp.dot(x_ref[...], w_ref[...], preferred_element_type=F32).astype(o_ref.dtype)


def _matmul(x, w, out_dtype, name):
    M, K = x.shape
    N = w.shape[1]
    tm, tn = _tile(M, 1024), _tile(N, 1024)
    return pl.pallas_call(
        _mm_kernel,
        grid=(M // tm, N // tn),
        in_specs=[pl.BlockSpec((tm, K), lambda i, j: (i, 0)),
                  pl.BlockSpec((K, tn), lambda i, j: (0, j))],
        out_specs=pl.BlockSpec((tm, tn), lambda i, j: (i, j)),
        out_shape=jax.ShapeDtypeStruct((M, N), out_dtype),
        compiler_params=_params(("parallel", "arbitrary")),
        name=name,
    )(x, w)


def _mm_nt_kernel(x_ref, wt_ref, *rest, with_ada):
    o_ref = rest[-2] if with_ada else rest[-1]
    o_ref[...] = lax.dot_general(x_ref[...], wt_ref[...], (((1,), (1,)), ((), ())),
                                 preferred_element_type=F32).astype(o_ref.dtype)
    if with_ada:
        c_ref, wa_ref, ba_ref, _, ada_ref = rest
        first = jnp.logical_and(pl.program_id(0) == 0, pl.program_id(1) == 0)
        _ada_accumulate(c_ref, wa_ref, ba_ref, ada_ref, first)


def _matmul_nt(x, wt, out_dtype, name, ada=None):
    M, K = x.shape
    N = wt.shape[0]
    tm, tn = _tile(M, 1024), _tile(N, 1024)
    grid = (M // tm, N // tn)
    in_specs = [pl.BlockSpec((tm, K), lambda i, j: (i, 0)),
                pl.BlockSpec((tn, K), lambda i, j: (j, 0))]
    out_specs = [pl.BlockSpec((tm, tn), lambda i, j: (i, j))]
    out_shape = [jax.ShapeDtypeStruct((M, N), out_dtype)]
    args = [x, wt]
    if ada is not None:
        c_pad, w_ada, b_ada, layer = ada
        L, D, NA = w_ada.shape
        R = c_pad.shape[0]
        steps = grid[0] * grid[1]
        rk = D // steps
        assert rk * steps == D and rk % 8 == 0
        c_steps = c_pad.reshape(R, steps, rk).swapaxes(0, 1)
        step = lambda i, j: i * grid[1] + j
        in_specs += [pl.BlockSpec((None, R, rk), lambda i, j: (step(i, j), 0, 0)),
                     pl.BlockSpec((None, rk, NA), lambda i, j: (layer, step(i, j), 0)),
                     pl.BlockSpec((None, 1, NA), lambda i, j: (layer, 0, 0))]
        out_specs.append(pl.BlockSpec((R, NA), lambda i, j: (0, 0)))
        out_shape.append(jax.ShapeDtypeStruct((R, NA), F32))
        args += [c_steps, w_ada, b_ada.reshape(L, 1, NA)]
    outs = pl.pallas_call(
        functools.partial(_mm_nt_kernel, with_ada=ada is not None),
        grid=grid,
        in_specs=in_specs,
        out_specs=out_specs,
        out_shape=out_shape,
        compiler_params=_params(("arbitrary", "arbitrary") if ada is not None else ("parallel", "arbitrary")),
        name=name,
    )(*args)
    return outs if ada is not None else (outs[0], None)


def _ada_can_ride(T, D):
    steps = (T // _tile(T, 1024)) * ((COL_GATE + 2 * D) // 1024)
    return D % steps == 0 and (D // steps) % 8 == 0


def _sb_kernel(q_ref, k_ref, v_ref, *rest, blk, heads, n_cast, relayout):
    n_in = n_cast + relayout
    wf_refs, o_ref, wb_refs = rest[:n_in], rest[n_in], rest[n_in + 1:2 * n_in + 1]
    acc_ref, carry_ref = rest[2 * n_in + 1:]
    for wf_ref, wb_ref in zip(wf_refs[:n_cast], wb_refs):
        wb_ref[...] = wf_ref[...].astype(BF16)
    if relayout:
        step = (pl.program_id(0) * pl.num_programs(1) + pl.program_id(1)) * pl.num_programs(2) + pl.program_id(2)
        _w_in_block(wf_refs[-1], wb_refs[-1], step * wb_refs[-1].shape[0])
    qi = pl.program_id(2)
    row = lax.broadcasted_iota(jnp.int32, (blk, blk), 0)
    col = lax.broadcasted_iota(jnp.int32, (blk, blk), 1)
    neg_suffix = jnp.where(row >= col, -1.0, 0.0).astype(BF16)
    strict = col < row

    def lanes(g):
        return slice(g * HEAD_DIM, (g + 1) * HEAD_DIM)

    def scores(off, g):
        k = k_ref[pl.ds(off, blk), lanes(g)]
        z = lax.dot_general(q_ref[:, lanes(g)], k, (((1,), (1,)), ((), ())),
                            preferred_element_type=F32)
        softplus = jnp.maximum(z, 0.0) + jnp.log(1.0 + jnp.exp2(jnp.abs(z) * (-LOG2E)))
        return z, softplus

    def weighted_values(off, g, w):
        return jnp.dot(w.astype(BF16), v_ref[pl.ds(off, blk), lanes(g)], preferred_element_type=F32)

    def suffix_sums(sp):
        return jnp.dot(sp.astype(BF16), neg_suffix, preferred_element_type=F32)

    hs = range(heads)

    off = pl.multiple_of(qi * blk, blk)
    zs = [scores(off, g) for g in hs]
    sps = [jnp.where(strict, sp, 0.0) for _, sp in zs]
    css = [suffix_sums(sp) for sp in sps]
    ws = [jnp.where(strict, jnp.exp2((z + cs) * LOG2E), 0.0) for (z, _), cs in zip(zs, css)]
    for g in hs:
        acc_ref[g] = weighted_values(off, g, ws[g])
        carry_ref[g] = -jnp.sum(sps[g], axis=1, keepdims=True)

    def body(state):
        n, _ = state
        off = pl.multiple_of((qi - 1 - n) * blk, blk)
        zs = [scores(off, g) for g in hs]
        css = [suffix_sums(sp) for _, sp in zs]
        ws = [jnp.exp2((z + cs + carry_ref[g]) * LOG2E) for g, ((z, _), cs) in enumerate(zip(zs, css))]
        for g in hs:
            acc_ref[g] += weighted_values(off, g, ws[g])
            carry_ref[g] -= jnp.sum(zs[g][1], axis=1, keepdims=True)
        return n + 1, jnp.max(carry_ref[...]) > DEAD_LOG_MASS

    lax.while_loop(lambda st: jnp.logical_and(st[0] < qi, st[1]), body, (jnp.int32(0), True))
    for g in range(heads):
        o_ref[:, lanes(g)] = acc_ref[g].astype(o_ref.dtype)


def _slab_specs(w, layer, grid):
    _, K, N = w.shape
    steps = grid[0] * grid[1] * grid[2]
    rows = K // steps
    assert rows * steps == K and rows % 16 == 0
    index = lambda b, h, i: (layer, (b * grid[1] + h) * grid[2] + i, 0)
    spec = pl.BlockSpec((None, rows, N), index)
    return spec, pl.BlockSpec((rows, N), lambda b, h, i: index(b, h, i)[1:]), jax.ShapeDtypeStruct((K, N), BF16)


def _sb_attention(proj3d, w_casts, layer, w_in_next=None):
    B, S, _ = proj3d.shape
    blk = _tile(S, 256)
    G = SB_HEAD_GROUP
    gw = G * HEAD_DIM
    grid = (B, HEADS // G, S // blk)
    steps = grid[0] * grid[1] * grid[2]
    step = lambda b, h, i: (b * grid[1] + h) * grid[2] + i
    wf_specs, wb_specs, wb_shapes = map(list, zip(*[_slab_specs(w, layer, grid) for w in w_casts]))
    args = [proj3d, proj3d, proj3d, *w_casts]
    if w_in_next is not None:
        wt, nxt = w_in_next
        src_spec, rows, n_pad = _w_in_window(wt, steps, lambda b, h, i: (nxt, step(b, h, i)))
        wf_specs.append(src_spec)
        wb_specs.append(pl.BlockSpec((rows, wt.shape[2]), lambda b, h, i: (step(b, h, i), 0)))
        wb_shapes.append(jax.ShapeDtypeStruct((n_pad, wt.shape[2]), BF16))
        args.append(wt)
    q_spec = pl.BlockSpec((None, blk, gw), lambda b, h, i: (b, i, COL_Q // gw + h))
    k_spec = pl.BlockSpec((None, S, gw), lambda b, h, i: (b, 0, COL_K // gw + h))
    v_spec = pl.BlockSpec((None, S, gw), lambda b, h, i: (b, 0, COL_V // gw + h))
    outs = pl.pallas_call(
        functools.partial(_sb_kernel, blk=blk, heads=G, n_cast=len(w_casts), relayout=w_in_next is not None),
        grid=grid,
        in_specs=[q_spec, k_spec, v_spec, *wf_specs],
        out_specs=[pl.BlockSpec((None, blk, gw), lambda b, h, i: (b, i, h)), *wb_specs],
        out_shape=[jax.ShapeDtypeStruct((B, S, ATT_WIDTH), BF16), *wb_shapes],
        scratch_shapes=[pltpu.VMEM((G, blk, HEAD_DIM), F32), pltpu.VMEM((G, blk, 1), F32)],
        compiler_params=_params(("parallel", "parallel", "arbitrary")),
        name="stick_breaking_attention",
    )(*args)
    return outs[0], outs[1:]


def _rope_rotate(xs, cos, sin_signed):
    half = ROPE_DIM // 2
    swapped = pltpu.roll(xs, half, 1) + pltpu.roll(xs, LANE - half, 1)
    return xs * cos + swapped * sin_signed


def _mla_prep_kernel(qd_ref, kvd_ref, gq_ref, gkv_ref, wq_ref, wkv_ref, cos_ref, sin_ref,
                     qn_ref, qr_ref, kn_ref, vt_ref, kr_ref, *, scale):
    cos = cos_ref[...]
    sin = sin_ref[...]
    c_q = _rms(qd_ref[...].astype(F32), gq_ref[...]).astype(BF16)
    yq = jnp.dot(c_q, wq_ref[...], preferred_element_type=F32)
    qn_ref[...] = (yq[:, :ATT_WIDTH] * scale).astype(BF16)
    for h in range(HEADS):
        lo = ATT_WIDTH + h * LANE
        qr_ref[:, h * LANE:(h + 1) * LANE] = (
            _rope_rotate(yq[:, lo:lo + LANE], cos, sin) * scale).astype(BF16)
    kvd = kvd_ref[...].astype(F32)
    c_kv = _rms(kvd[:, :KV_RANK], gkv_ref[...]).astype(BF16)
    ykv = jnp.dot(c_kv, wkv_ref[...], preferred_element_type=F32)
    kn_ref[...] = ykv[:, :ATT_WIDTH].astype(BF16)
    vt_ref[...] = ykv[:, ATT_WIDTH:].T.astype(BF16)
    kr_ref[...] = _rope_rotate(kvd[:, KV_RANK:KV_RANK + LANE], cos, sin).astype(BF16)


def _mla_prep(proj, g_q, g_kv, wq, wkv, cos_t, sin_t, B, S):
    T = proj.shape[0]
    tm = _tile(S, MLA_BLOCK)
    tpb = S // tm
    row = lambda w, c: pl.BlockSpec((tm, w), lambda i: (i, c))
    full = lambda a: pl.BlockSpec(a.shape, lambda i: (0, 0))
    wide = jax.ShapeDtypeStruct((T, ATT_WIDTH), BF16)
    return pl.pallas_call(
        functools.partial(_mla_prep_kernel, scale=LOG2E * (HEAD_DIM + ROPE_DIM) ** -0.5),
        grid=(T // tm,),
        in_specs=[row(1024, COL_QD // 1024), row(1024, COL_KVD // 1024),
                  pl.BlockSpec((1, Q_RANK), lambda i: (0, 0)),
                  pl.BlockSpec((1, KV_RANK), lambda i: (0, 0)),
                  full(wq), full(wkv), row(LANE, 0), row(LANE, 0)],
        out_specs=[row(ATT_WIDTH, 0)] * 3
                  + [pl.BlockSpec((None, None, ATT_WIDTH, tm), lambda i: (i // tpb, i % tpb, 0, 0)),
                     row(LANE, 0)],
        out_shape=[wide, wide, wide, jax.ShapeDtypeStruct((B, tpb, ATT_WIDTH, tm), BF16),
                   jax.ShapeDtypeStruct((T, LANE), BF16)],
        compiler_params=_params(("parallel",)),
        name="latent_projections",
    )(proj, proj, g_q.reshape(1, Q_RANK), g_kv.reshape(1, KV_RANK), wq, wkv, cos_t, sin_t)


def _mla_kernel(qn_ref, qr_ref, kn_ref, kr_ref, vt_ref, *rest, blk, heads, n_cast):
    wf_refs, o_ref, wb_refs = rest[:n_cast], rest[n_cast], rest[n_cast + 1:2 * n_cast + 1]
    acc_ref, m_ref = rest[2 * n_cast + 1:]
    for wf_ref, wb_ref in zip(wf_refs, wb_refs):
        wb_ref[...] = wf_ref[...].astype(BF16)
    qi = pl.program_id(2)
    acc_ref[...] = jnp.zeros_like(acc_ref)
    m_ref[...] = jnp.full_like(m_ref, NEG_BIG)

    def lanes(g):
        return slice(g * HEAD_DIM, (g + 1) * HEAD_DIM)

    def step(kb, n_kb, mask):
        width = n_kb * blk
        off = pl.multiple_of(kb * blk, blk)
        kr = kr_ref[pl.ds(off, width), :]
        ones = jnp.ones((ONES_ROWS, width), BF16)
        ss = []
        for g in range(heads):
            q = jnp.concatenate([qn_ref[:, lanes(g)], qr_ref[:, lanes(g)]], axis=1)
            k = jnp.concatenate([kn_ref[pl.ds(off, width), lanes(g)], kr], axis=1)
            s = lax.dot_general(k, q, (((1,), (1,)), ((), ())), preferred_element_type=F32)
            ss.append(s if mask is None else jnp.where(mask, s, NEG_BIG))
        ps, alphas = [], []
        for g, s in enumerate(ss):
            m_old = m_ref[g][0:1, :]
            m_new = jnp.maximum(m_old, jnp.max(s, axis=0, keepdims=True))
            ps.append(jnp.exp2(s - m_new).astype(BF16))
            alphas.append(jnp.exp2(m_old - m_new))
            m_ref[g] = jnp.broadcast_to(m_new, m_ref.shape[1:])
        for g in range(heads):
            vt = jnp.concatenate([vt_ref[kb + n, lanes(g), :] for n in range(n_kb)] , axis=1)
            vt = jnp.concatenate([vt, ones], axis=0)
            acc_ref[g] = acc_ref[g] * alphas[g] + jnp.dot(vt, ps[g], preferred_element_type=F32)

    def body(j, _):
        step(2 * j, 2, None)
        return 0

    lax.fori_loop(0, qi // 2, body, 0)

    @pl.when(qi % 2 == 1)
    def _():
        step(qi - 1, 1, None)

    key_chunk = lax.broadcasted_iota(jnp.int32, (blk, blk), 0) // CHUNK
    query_chunk = lax.broadcasted_iota(jnp.int32, (blk, blk), 1) // CHUNK
    step(qi, 1, key_chunk <= query_chunk)
    for g in range(heads):
        acc = acc_ref[g]
        out_t = acc[:HEAD_DIM, :] / acc[HEAD_DIM:HEAD_DIM + 1, :]
        o_ref[:, lanes(g)] = out_t.T.astype(o_ref.dtype)


def _mla_attention(qn, qr, kn, kr, vt, B, S, w_casts, layer):
    blk = _tile(S, MLA_BLOCK)
    G = MLA_HEAD_GROUP
    gw = G * HEAD_DIM
    grid = (B, HEADS // G, S // blk)
    wf_specs, wb_specs, wb_shapes = zip(*[_slab_specs(w, layer, grid) for w in w_casts])
    sh3 = lambda a: a.reshape(B, S, a.shape[-1])
    once = pl.Buffered(1)
    q_spec = pl.BlockSpec((None, blk, gw), lambda b, h, i: (b, i, h))
    k_spec = pl.BlockSpec((None, S, gw), lambda b, h, i: (b, 0, h), pipeline_mode=once)
    kr_spec = pl.BlockSpec((None, S, LANE), lambda b, h, i: (b, 0, 0), pipeline_mode=once)
    vt_spec = pl.BlockSpec((None, S // blk, gw, blk), lambda b, h, i: (b, 0, h, 0), pipeline_mode=once)
    outs = pl.pallas_call(
        functools.partial(_mla_kernel, blk=blk, heads=G, n_cast=len(w_casts)),
        grid=grid,
        in_specs=[q_spec, q_spec, k_spec, kr_spec, vt_spec, *wf_specs],
        out_specs=[q_spec, *wb_specs],
        out_shape=[jax.ShapeDtypeStruct((B, S, ATT_WIDTH), BF16), *wb_shapes],
        scratch_shapes=[pltpu.VMEM((G, HEAD_DIM + ONES_ROWS, blk), F32), pltpu.VMEM((G, 8, blk), F32)],
        compiler_params=_params(("parallel", "parallel", "arbitrary")),
        name="latent_attention",
    )(sh3(qn), sh3(qr), sh3(kn), sh3(kr), vt, *w_casts)
    return outs[0], outs[1:]


def _sigmoid(x):
    return 1.0 / (1.0 + jnp.exp(-x))


def _merge_kernel(osb_ref, omla_ref, wsb_ref, wmla_ref, gsb_ref, gmla_ref, o_ref):
    a = jnp.dot(osb_ref[...], wsb_ref[...], preferred_element_type=F32)
    b = jnp.dot(omla_ref[...], wmla_ref[...], preferred_element_type=F32)
    o_ref[...] = (_sigmoid(gsb_ref[...].astype(F32)) * a
                  + _sigmoid(gmla_ref[...].astype(F32)) * b).astype(o_ref.dtype)


def _merge(o_sb, o_mla, w_sb, w_mla, proj):
    T, W = o_sb.shape
    D = w_sb.shape[1]
    tm, tn = _tile(T, 512), _tile(D, 1024)
    gate0 = COL_GATE // tn
    act = pl.BlockSpec((tm, W), lambda i, j: (i, 0))
    wsp = pl.BlockSpec((W, tn), lambda i, j: (0, j))
    return pl.pallas_call(
        _merge_kernel,
        grid=(T // tm, D // tn),
        in_specs=[act, act, wsp, wsp,
                  pl.BlockSpec((tm, tn), lambda i, j: (i, gate0 + j)),
                  pl.BlockSpec((tm, tn), lambda i, j: (i, gate0 + D // tn + j))],
        out_specs=pl.BlockSpec((tm, tn), lambda i, j: (i, j)),
        out_shape=jax.ShapeDtypeStruct((T, D), BF16),
        compiler_params=_params(("parallel", "arbitrary")),
        name="gated_branch_merge",
    )(o_sb, o_mla, w_sb, w_mla, proj, proj)


def _mlp_kernel(h_ref, wu_ref, wd_ref, o_ref, acc_ref):
    f = pl.program_id(1)

    @pl.when(f == 0)
    def _():
        acc_ref[...] = jnp.zeros_like(acc_ref)

    u = jnp.dot(h_ref[...], wu_ref[...], preferred_element_type=F32)
    a = jnp.square(jnp.maximum(u, 0.0)).astype(BF16)
    acc_ref[...] += jnp.dot(a, wd_ref[...], preferred_element_type=F32)

    @pl.when(f == pl.num_programs(1) - 1)
    def _():
        o_ref[...] = acc_ref[...].astype(o_ref.dtype)


def _mlp(h, w_up, w_down):
    T, D = h.shape
    F = w_up.shape[1]
    tm, tf = _tile(T, 512), _tile(F, 512)
    return pl.pallas_call(
        _mlp_kernel,
        grid=(T // tm, F // tf),
        in_specs=[pl.BlockSpec((tm, D), lambda i, f: (i, 0)),
                  pl.BlockSpec((D, tf), lambda i, f: (0, f)),
                  pl.BlockSpec((tf, D), lambda i, f: (f, 0))],
        out_specs=pl.BlockSpec((tm, D), lambda i, f: (i, 0)),
        out_shape=jax.ShapeDtypeStruct((T, D), BF16),
        scratch_shapes=[pltpu.VMEM((tm, D), F32)],
        compiler_params=_params(("parallel", "arbitrary")),
        name="relu2_mlp",
    )(h, w_up, w_down)


KVD_END = COL_KVD + KV_RANK + ROPE_DIM
GATE_SHIFT = COL_GATE - KVD_END


def _w_in_block(w_ref, o_ref, row0):
    r = row0 + lax.broadcasted_iota(jnp.int32, o_ref.shape, 0)
    w = w_ref[...]
    w = jnp.where(r < ATT_WIDTH, w * (HEAD_DIM ** -0.5), w)
    keep = jnp.where(r < KVD_END, 1, 0) + jnp.where(r >= COL_GATE, 1, 0)
    o_ref[...] = jnp.where(keep > 0, w, 0.0).astype(BF16)


def _w_in_window(wt, n_blocks, where):
    _, n_src, D = wt.shape
    n_pad = n_src + GATE_SHIFT
    rows = n_pad // n_blocks
    assert rows * n_blocks == n_pad and rows % 16 == 0
    unit = 8
    while rows % (2 * unit) == 0 and GATE_SHIFT % (2 * unit) == 0:
        unit *= 2
    assert all(not (j * rows < KVD_END and (j + 1) * rows > COL_GATE) for j in range(n_blocks))

    def src(*ids):
        layer, j = where(*ids)
        shift = jnp.where(j * rows < KVD_END, 0, GATE_SHIFT // unit)
        return layer, (j * (rows // unit) - shift) * unit, 0

    return pl.BlockSpec((None, pl.Element(rows), pl.Element(D)), src), rows, n_pad


def _w_in_kernel(w_ref, o_ref):
    _w_in_block(w_ref, o_ref, pl.program_id(0) * o_ref.shape[0])


def _prep_w_in(wt, layer):
    n_blocks = (wt.shape[1] + GATE_SHIFT) // 512
    src_spec, rows, n_pad = _w_in_window(wt, n_blocks, lambda j: (layer, j))
    return pl.pallas_call(
        _w_in_kernel,
        grid=(n_blocks,),
        in_specs=[src_spec],
        out_specs=pl.BlockSpec((rows, wt.shape[2]), lambda j: (j, 0)),
        out_shape=jax.ShapeDtypeStruct((n_pad, wt.shape[2]), BF16),
        compiler_params=_params(("parallel",)),
        name="input_weight_layout",
    )(wt)


def _prep_w_uq(w):
    w = w.reshape(Q_RANK, HEADS, HEAD_DIM + ROPE_DIM)
    nope = w[:, :, :HEAD_DIM].reshape(Q_RANK, ATT_WIDTH)
    rope = jnp.pad(w[:, :, HEAD_DIM:], ((0, 0), (0, 0), (0, LANE - ROPE_DIM))).reshape(Q_RANK, HEADS * LANE)
    return jnp.concatenate([nope, rope], axis=1).astype(BF16)


def _prep_w_ukv(w):
    w = w.reshape(KV_RANK, HEADS, 2 * HEAD_DIM)
    k = w[:, :, :HEAD_DIM].reshape(KV_RANK, ATT_WIDTH)
    v = w[:, :, HEAD_DIM:].reshape(KV_RANK, ATT_WIDTH)
    return jnp.concatenate([k, v], axis=1).astype(BF16)


def kernel(x, c, positions, w_ada, b_ada, g_pre_mix, g_post_mix, g_pre_mlp, g_post_mlp, w_in,
           g_q_lora, w_uq, g_kv_lora, w_ukv, w_o_sb, w_o_mla, w_out, w_up, w_down):
    B, S, D = x.shape
    L = w_ada.shape[0]
    T = B * S
    assert S % CHUNK == 0 and D % 512 == 0

    rows = 16
    c_pad = jnp.zeros((rows, D), F32).at[:B].set(c)
    ride_ada = _ada_can_ride(T, D)
    six = lambda a: a[:B].reshape(B, 6, 1, D)
    ada = [six(_ada(c_pad, w_ada, b_ada, l)) if (l == 0 or not ride_ada) else None for l in range(L)]
    cos_t, sin_t = _rope_tables(positions)

    w_in_t = jnp.swapaxes(w_in, 1, 2)
    w_in_b = _prep_w_in(w_in_t, 0)

    x2d = x.reshape(T, D)
    h = _prenorm(x2d, g_pre_mix[0], ada[0], 0, 1, S)
    for l in range(L):
        more = l + 1 < L
        proj, ada_next = _matmul_nt(h, w_in_b, BF16, "input_projection",
                                    ada=(c_pad, w_ada, b_ada, l + 1) if more and ride_ada else None)
        if ada_next is not None:
            ada[l + 1] = six(ada_next)
        o_sb, sb_extra = _sb_attention(proj.reshape(B, S, -1), (w_o_sb, w_o_mla, w_out), l,
                                       w_in_next=(w_in_t, l + 1) if more else None)
        w_o_sb_b, w_o_mla_b, w_out_b = sb_extra[:3]
        if more:
            w_in_b = sb_extra[3]
        o_sb = o_sb.reshape(T, ATT_WIDTH)
        qn, qr, kn, vt, kr = _mla_prep(proj, g_q_lora[l], g_kv_lora[l], _prep_w_uq(w_uq[l]),
                                       _prep_w_ukv(w_ukv[l]), cos_t, sin_t, B, S)
        o_mla, (w_up_b, w_down_b) = _mla_attention(qn, qr, kn, kr, vt, B, S, (w_up, w_down), l)
        o_mla = o_mla.reshape(T, ATT_WIDTH)
        merged = _merge(o_sb, o_mla, w_o_sb_b, w_o_mla_b, proj)
        y = _matmul(merged, w_out_b, BF16, "output_projection")
        x2d, h = _post(y, x2d, g_post_mix[l], ada[l], 2, S, nxt=(g_pre_mlp[l], ada[l], 3, 4))
        y = _mlp(h, w_up_b, w_down_b)
        nxt = (g_pre_mix[l + 1], ada[l + 1], 0, 1) if more else None
        x2d, h = _post(y, x2d, g_post_mlp[l], ada[l], 5, S, nxt=nxt)
    return x2d.reshape(B, S, D)
```

```python
import functools

import jax
import jax.numpy as jnp
from jax import lax
from jax.experimental import pallas as pl
from jax.experimental.pallas import tpu as pltpu

F32 = jnp.float32
BF16 = jnp.bfloat16

HEADS = 16
HEAD_DIM = 128
ATT_WIDTH = HEADS * HEAD_DIM
ROPE_DIM = 64
Q_RANK = 1024
KV_RANK = 512
CHUNK = 64
ROPE_THETA = 10000.0
EPS = 1e-6
NEG_BIG = -1e30
LOG2E = 1.4426950408889634
DEAD_LOG_MASS = -104.0
SB_HEAD_GROUP = 8
MLA_HEAD_GROUP = 8
MLA_BLOCK = 256
ONES_ROWS = 16

COL_Q, COL_K, COL_V = 0, ATT_WIDTH, 2 * ATT_WIDTH
COL_QD = 3 * ATT_WIDTH
COL_KVD = COL_QD + Q_RANK
COL_GATE = COL_KVD + 1024

V7X_VMEM_LIMIT_BYTES = 56 * 1024 * 1024
LANE = 128


def _params(sem):
    return pltpu.CompilerParams(dimension_semantics=sem, vmem_limit_bytes=V7X_VMEM_LIMIT_BYTES)


def _tile(n, pref):
    if n <= pref:
        return n
    t = (pref // LANE) * LANE
    while t >= LANE:
        if n % t == 0:
            return t
        t -= LANE
    raise ValueError(f"no tile for {n}")


def _silu(c):
    return c * (1.0 / (1.0 + jnp.exp(-c)))


def _ada_accumulate(c_ref, w_ref, b_ref, o_ref, first):
    @pl.when(first)
    def _():
        o_ref[...] = jnp.broadcast_to(b_ref[...], o_ref.shape)

    o_ref[...] += jnp.dot(_silu(c_ref[...]).astype(BF16), w_ref[...].astype(BF16),
                          preferred_element_type=F32)


def _ada_kernel(c_ref, w_ref, b_ref, o_ref):
    _ada_accumulate(c_ref, w_ref, b_ref, o_ref, pl.program_id(0) == 0)


def _ada(c_pad, w_ada, b_ada, layer):
    L, D, N = w_ada.shape
    R = c_pad.shape[0]
    tk = _tile(D, 128)
    return pl.pallas_call(
        _ada_kernel,
        grid=(D // tk,),
        in_specs=[pl.BlockSpec((R, tk), lambda k: (0, k)),
                  pl.BlockSpec((None, tk, N), lambda k: (layer, k, 0)),
                  pl.BlockSpec((None, 1, N), lambda k: (layer, 0, 0))],
        out_specs=pl.BlockSpec((R, N), lambda k: (0, 0)),
        out_shape=jax.ShapeDtypeStruct((R, N), F32),
        compiler_params=_params(("arbitrary",)),
        name="ada_proj",
    )(c_pad, w_ada, b_ada.reshape(L, 1, N))


def _rope_table_kernel(pos_ref, inv_ref, sgn_ref, cos_ref, sin_ref):
    ang = pos_ref[...].astype(F32) * inv_ref[...]
    cos_ref[...] = jnp.cos(ang)
    sin_ref[...] = jnp.sin(ang) * sgn_ref[...]


def _rope_tables(positions):
    T = positions.size
    half = ROPE_DIM // 2
    inv_freq = ROPE_THETA ** (-jnp.arange(half, dtype=F32) / half)
    zeros = jnp.zeros((LANE - ROPE_DIM,), F32)
    inv = jnp.concatenate([inv_freq, inv_freq, zeros]).reshape(1, LANE)
    sgn = jnp.concatenate([-jnp.ones((half,), F32), jnp.ones((half,), F32), zeros]).reshape(1, LANE)
    tm = _tile(T, 1024)
    tab = jax.ShapeDtypeStruct((T, LANE), F32)
    return pl.pallas_call(
        _rope_table_kernel,
        grid=(T // tm,),
        in_specs=[pl.BlockSpec((tm, 1), lambda i: (i, 0)),
                  pl.BlockSpec((1, LANE), lambda i: (0, 0)),
                  pl.BlockSpec((1, LANE), lambda i: (0, 0))],
        out_specs=[pl.BlockSpec((tm, LANE), lambda i: (i, 0))] * 2,
        out_shape=[tab, tab],
        compiler_params=_params(("parallel",)),
        name="rope_tables",
    )(positions.reshape(T, 1), inv, sgn)


def _rms(x, g):
    return x * lax.rsqrt(jnp.mean(x * x, axis=-1, keepdims=True) + EPS) * g


def _ada_spec(D, j, tiles_per_batch):
    return pl.BlockSpec((None, None, 1, D), lambda i: (i // tiles_per_batch, j, 0, 0))


def _prenorm_kernel(x_ref, g_ref, sh_ref, sc_ref, h_ref):
    h = _rms(x_ref[...], g_ref[...])
    h_ref[...] = (h * (1.0 + sc_ref[...]) + sh_ref[...]).astype(BF16)


def _prenorm(x2d, g, ada_l, j_shift, j_scale, S):
    T, D = x2d.shape
    tm = _tile(S, 512)
    tpb = S // tm
    row = pl.BlockSpec((tm, D), lambda i: (i, 0))
    vec = pl.BlockSpec((1, D), lambda i: (0, 0))
    return pl.pallas_call(
        _prenorm_kernel,
        grid=(T // tm,),
        in_specs=[row, vec, _ada_spec(D, j_shift, tpb), _ada_spec(D, j_scale, tpb)],
        out_specs=row,
        out_shape=jax.ShapeDtypeStruct((T, D), BF16),
        compiler_params=_params(("parallel",)),
        name="prenorm",
    )(x2d, g.reshape(1, D), ada_l, ada_l)


def _post_kernel(y_ref, x_ref, gpost_ref, gate_ref, *rest, with_next):
    y = y_ref[...].astype(F32)
    xn = x_ref[...] + gate_ref[...] * _rms(y, gpost_ref[...])
    if with_next:
        gpre_ref, sh_ref, sc_ref, xo_ref, h_ref = rest
        h = _rms(xn, gpre_ref[...])
        h_ref[...] = (h * (1.0 + sc_ref[...]) + sh_ref[...]).astype(BF16)
    else:
        (xo_ref,) = rest
    xo_ref[...] = xn


def _post(y, x2d, g_post, ada_l, j_gate, S, nxt=None):
    T, D = x2d.shape
    tm = _tile(S, 256)
    tpb = S // tm
    row = pl.BlockSpec((tm, D), lambda i: (i, 0))
    vec = pl.BlockSpec((1, D), lambda i: (0, 0))
    in_specs = [row, row, vec, _ada_spec(D, j_gate, tpb)]
    args = [y, x2d, g_post.reshape(1, D), ada_l]
    out_specs = [row]
    out_shape = [jax.ShapeDtypeStruct((T, D), F32)]
    if nxt is not None:
        g_pre, ada_n, j_shift, j_scale = nxt
        in_specs += [vec, _ada_spec(D, j_shift, tpb), _ada_spec(D, j_scale, tpb)]
        args += [g_pre.reshape(1, D), ada_n, ada_n]
        out_specs.append(row)
        out_shape.append(jax.ShapeDtypeStruct((T, D), BF16))
    outs = pl.pallas_call(
        functools.partial(_post_kernel, with_next=nxt is not None),
        grid=(T // tm,),
        in_specs=in_specs,
        out_specs=out_specs,
        out_shape=out_shape,
        compiler_params=_params(("parallel",)),
        name="post_norm_residual",
    )(*args)
    return outs if nxt is not None else (outs[0], None)


def _mm_kernel(x_ref, w_ref, o_ref):
    o_ref[...] = jnp.dot(x_ref[...], w_ref[...], preferred_element_type=F32).astype(o_ref.dtype)


def _matmul(x, w, out_dtype, name):
    M, K = x.shape
    N = w.shape[1]
    tm, tn = _tile(M, 1024), _tile(N, 1024)
    return pl.pallas_call(
        _mm_kernel,
        grid=(M // tm, N // tn),
        in_specs=[pl.BlockSpec((tm, K), lambda i, j: (i, 0)),
                  pl.BlockSpec((K, tn), lambda i, j: (0, j))],
        out_specs=pl.BlockSpec((tm, tn), lambda i, j: (i, j)),
        out_shape=jax.ShapeDtypeStruct((M, N), out_dtype),
        compiler_params=_params(("parallel", "arbitrary")),
        name=name,
    )(x, w)


def _mm_nt_kernel(x_ref, wt_ref, *rest, with_ada):
    o_ref = rest[-2] if with_ada else rest[-1]
    o_ref[...] = lax.dot_general(x_ref[...], wt_ref[...], (((1,), (1,)), ((), ())),
                                 preferred_element_type=F32).astype(o_ref.dtype)
    if with_ada:
        c_ref, wa_ref, ba_ref, _, ada_ref = rest
        first = jnp.logical_and(pl.program_id(0) == 0, pl.program_id(1) == 0)
        _ada_accumulate(c_ref, wa_ref, ba_ref, ada_ref, first)


def _matmul_nt(x, wt, out_dtype, name, ada=None):
    M, K = x.shape
    N = wt.shape[0]
    tm, tn = _tile(M, 1024), _tile(N, 1024)
    grid = (M // tm, N // tn)
    in_specs = [pl.BlockSpec((tm, K), lambda i, j: (i, 0)),
                pl.BlockSpec((tn, K), lambda i, j: (j, 0))]
    out_specs = [pl.BlockSpec((tm, tn), lambda i, j: (i, j))]
    out_shape = [jax.ShapeDtypeStruct((M, N), out_dtype)]
    args = [x, wt]
    if ada is not None:
        c_pad, w_ada, b_ada, layer = ada
        L, D, NA = w_ada.shape
        R = c_pad.shape[0]
        steps = grid[0] * grid[1]
        rk = D // steps
        assert rk * steps == D and rk % 8 == 0
        c_steps = c_pad.reshape(R, steps, rk).swapaxes(0, 1)
        step = lambda i, j: i * grid[1] + j
        in_specs += [pl.BlockSpec((None, R, rk), lambda i, j: (step(i, j), 0, 0)),
                     pl.BlockSpec((None, rk, NA), lambda i, j: (layer, step(i, j), 0)),
                     pl.BlockSpec((None, 1, NA), lambda i, j: (layer, 0, 0))]
        out_specs.append(pl.BlockSpec((R, NA), lambda i, j: (0, 0)))
        out_shape.append(jax.ShapeDtypeStruct((R, NA), F32))
        args += [c_steps, w_ada, b_ada.reshape(L, 1, NA)]
    outs = pl.pallas_call(
        functools.partial(_mm_nt_kernel, with_ada=ada is not None),
        grid=grid,
        in_specs=in_specs,
        out_specs=out_specs,
        out_shape=out_shape,
        compiler_params=_params(("arbitrary", "arbitrary") if ada is not None else ("parallel", "arbitrary")),
        name=name,
    )(*args)
    return outs if ada is not None else (outs[0], None)


def _ada_can_ride(T, D):
    steps = (T // _tile(T, 1024)) * ((COL_GATE + 2 * D) // 1024)
    return D % steps == 0 and (D // steps) % 8 == 0


def _sb_kernel(q_ref, k_ref, v_ref, *rest, blk, heads, n_cast, relayout):
    n_in = n_cast + relayout
    wf_refs, o_ref, wb_refs = rest[:n_in], rest[n_in], rest[n_in + 1:2 * n_in + 1]
    acc_ref, carry_ref = rest[2 * n_in + 1:]
    for wf_ref, wb_ref in zip(wf_refs[:n_cast], wb_refs):
        wb_ref[...] = wf_ref[...].astype(BF16)
    if relayout:
        step = (pl.program_id(0) * pl.num_programs(1) + pl.program_id(1)) * pl.num_programs(2) + pl.program_id(2)
        _w_in_block(wf_refs[-1], wb_refs[-1], step * wb_refs[-1].shape[0])
    qi = pl.program_id(2)
    row = lax.broadcasted_iota(jnp.int32, (blk, blk), 0)
    col = lax.broadcasted_iota(jnp.int32, (blk, blk), 1)
    neg_suffix = jnp.where(row >= col, -1.0, 0.0).astype(BF16)
    strict = col < row

    def lanes(g):
        return slice(g * HEAD_DIM, (g + 1) * HEAD_DIM)

    def scores(off, g):
        k = k_ref[pl.ds(off, blk), lanes(g)]
        z = lax.dot_general(q_ref[:, lanes(g)], k, (((1,), (1,)), ((), ())),
                            preferred_element_type=F32)
        softplus = jnp.maximum(z, 0.0) + jnp.log(1.0 + jnp.exp2(jnp.abs(z) * (-LOG2E)))
        return z, softplus

    def weighted_values(off, g, w):
        return jnp.dot(w.astype(BF16), v_ref[pl.ds(off, blk), lanes(g)], preferred_element_type=F32)

    def suffix_sums(sp):
        return jnp.dot(sp.astype(BF16), neg_suffix, preferred_element_type=F32)

    hs = range(heads)

    off = pl.multiple_of(qi * blk, blk)
    zs = [scores(off, g) for g in hs]
    sps = [jnp.where(strict, sp, 0.0) for _, sp in zs]
    css = [suffix_sums(sp) for sp in sps]
    ws = [jnp.where(strict, jnp.exp2((z + cs) * LOG2E), 0.0) for (z, _), cs in zip(zs, css)]
    for g in hs:
        acc_ref[g] = weighted_values(off, g, ws[g])
        carry_ref[g] = -jnp.sum(sps[g], axis=1, keepdims=True)

    def body(state):
        n, _ = state
        off = pl.multiple_of((qi - 1 - n) * blk, blk)
        zs = [scores(off, g) for g in hs]
        css = [suffix_sums(sp) for _, sp in zs]
        ws = [jnp.exp2((z + cs + carry_ref[g]) * LOG2E) for g, ((z, _), cs) in enumerate(zip(zs, css))]
        for g in hs:
            acc_ref[g] += weighted_values(off, g, ws[g])
            carry_ref[g] -= jnp.sum(zs[g][1], axis=1, keepdims=True)
        return n + 1, jnp.max(carry_ref[...]) > DEAD_LOG_MASS

    lax.while_loop(lambda st: jnp.logical_and(st[0] < qi, st[1]), body, (jnp.int32(0), True))
    for g in range(heads):
        o_ref[:, lanes(g)] = acc_ref[g].astype(o_ref.dtype)


def _slab_specs(w, layer, grid):
    _, K, N = w.shape
    steps = grid[0] * grid[1] * grid[2]
    rows = K // steps
    assert rows * steps == K and rows % 16 == 0
    index = lambda b, h, i: (layer, (b * grid[1] + h) * grid[2] + i, 0)
    spec = pl.BlockSpec((None, rows, N), index)
    return spec, pl.BlockSpec((rows, N), lambda b, h, i: index(b, h, i)[1:]), jax.ShapeDtypeStruct((K, N), BF16)


def _sb_attention(proj3d, w_casts, layer, w_in_next=None):
    B, S, _ = proj3d.shape
    blk = _tile(S, 256)
    G = SB_HEAD_GROUP
    gw = G * HEAD_DIM
    grid = (B, HEADS // G, S // blk)
    steps = grid[0] * grid[1] * grid[2]
    step = lambda b, h, i: (b * grid[1] + h) * grid[2] + i
    wf_specs, wb_specs, wb_shapes = map(list, zip(*[_slab_specs(w, layer, grid) for w in w_casts]))
    args = [proj3d, proj3d, proj3d, *w_casts]
    if w_in_next is not None:
        wt, nxt = w_in_next
        src_spec, rows, n_pad = _w_in_window(wt, steps, lambda b, h, i: (nxt, step(b, h, i)))
        wf_specs.append(src_spec)
        wb_specs.append(pl.BlockSpec((rows, wt.shape[2]), lambda b, h, i: (step(b, h, i), 0)))
        wb_shapes.append(jax.ShapeDtypeStruct((n_pad, wt.shape[2]), BF16))
        args.append(wt)
    q_spec = pl.BlockSpec((None, blk, gw), lambda b, h, i: (b, i, COL_Q // gw + h))
    once = pl.Buffered(1)
    k_spec = pl.BlockSpec((None, S, gw), lambda b, h, i: (b, 0, COL_K // gw + h), pipeline_mode=once)
    v_spec = pl.BlockSpec((None, S, gw), lambda b, h, i: (b, 0, COL_V // gw + h), pipeline_mode=once)
    outs = pl.pallas_call(
        functools.partial(_sb_kernel, blk=blk, heads=G, n_cast=len(w_casts), relayout=w_in_next is not None),
        grid=grid,
        in_specs=[q_spec, k_spec, v_spec, *wf_specs],
        out_specs=[pl.BlockSpec((None, blk, gw), lambda b, h, i: (b, i, h)), *wb_specs],
        out_shape=[jax.ShapeDtypeStruct((B, S, ATT_WIDTH), BF16), *wb_shapes],
        scratch_shapes=[pltpu.VMEM((G, blk, HEAD_DIM), F32), pltpu.VMEM((G, blk, 1), F32)],
        compiler_params=_params(("parallel", "parallel", "arbitrary")),
        name="stick_breaking_attention",
    )(*args)
    return outs[0], outs[1:]


def _rope_rotate(xs, cos, sin_signed):
    half = ROPE_DIM // 2
    swapped = pltpu.roll(xs, half, 1) + pltpu.roll(xs, LANE - half, 1)
    return xs * cos + swapped * sin_signed


def _mla_prep_kernel(qd_ref, kvd_ref, gq_ref, gkv_ref, wq_ref, wkv_ref, cos_ref, sin_ref,
                     qn_ref, qr_ref, kn_ref, vt_ref, kr_ref, *, scale):
    cos = cos_ref[...]
    sin = sin_ref[...]
    c_q = _rms(qd_ref[...].astype(F32), gq_ref[...]).astype(BF16)
    yq = jnp.dot(c_q, wq_ref[...], preferred_element_type=F32)
    qn_ref[...] = (yq[:, :ATT_WIDTH] * scale).astype(BF16)
    for h in range(HEADS):
        lo = ATT_WIDTH + h * LANE
        qr_ref[:, h * LANE:(h + 1) * LANE] = (
            _rope_rotate(yq[:, lo:lo + LANE], cos, sin) * scale).astype(BF16)
    kvd = kvd_ref[...].astype(F32)
    c_kv = _rms(kvd[:, :KV_RANK], gkv_ref[...]).astype(BF16)
    ykv = jnp.dot(c_kv, wkv_ref[...], preferred_element_type=F32)
    kn_ref[...] = ykv[:, :ATT_WIDTH].astype(BF16)
    vt_ref[...] = ykv[:, ATT_WIDTH:].T.astype(BF16)
    kr_ref[...] = _rope_rotate(kvd[:, KV_RANK:KV_RANK + LANE], cos, sin).astype(BF16)


def _mla_prep(proj, g_q, g_kv, wq, wkv, cos_t, sin_t, B, S):
    T = proj.shape[0]
    tm = _tile(S, MLA_BLOCK)
    tpb = S // tm
    row = lambda w, c: pl.BlockSpec((tm, w), lambda i: (i, c))
    full = lambda a: pl.BlockSpec(a.shape, lambda i: (0, 0))
    wide = jax.ShapeDtypeStruct((T, ATT_WIDTH), BF16)
    return pl.pallas_call(
        functools.partial(_mla_prep_kernel, scale=LOG2E * (HEAD_DIM + ROPE_DIM) ** -0.5),
        grid=(T // tm,),
        in_specs=[row(1024, COL_QD // 1024), row(1024, COL_KVD // 1024),
                  pl.BlockSpec((1, Q_RANK), lambda i: (0, 0)),
                  pl.BlockSpec((1, KV_RANK), lambda i: (0, 0)),
                  full(wq), full(wkv), row(LANE, 0), row(LANE, 0)],
        out_specs=[row(ATT_WIDTH, 0)] * 3
                  + [pl.BlockSpec((None, None, ATT_WIDTH, tm), lambda i: (i // tpb, i % tpb, 0, 0)),
                     row(LANE, 0)],
        out_shape=[wide, wide, wide, jax.ShapeDtypeStruct((B, tpb, ATT_WIDTH, tm), BF16),
                   jax.ShapeDtypeStruct((T, LANE), BF16)],
        compiler_params=_params(("parallel",)),
        name="latent_projections",
    )(proj, proj, g_q.reshape(1, Q_RANK), g_kv.reshape(1, KV_RANK), wq, wkv, cos_t, sin_t)


def _mla_kernel(qn_ref, qr_ref, kn_ref, kr_ref, vt_ref, *rest, blk, heads, n_cast):
    wf_refs, o_ref, wb_refs = rest[:n_cast], rest[n_cast], rest[n_cast + 1:2 * n_cast + 1]
    acc_ref, m_ref = rest[2 * n_cast + 1:]
    for wf_ref, wb_ref in zip(wf_refs, wb_refs):
        wb_ref[...] = wf_ref[...].astype(BF16)
    qi = pl.program_id(2)
    acc_ref[...] = jnp.zeros_like(acc_ref)
    m_ref[...] = jnp.full_like(m_ref, NEG_BIG)

    def lanes(g):
        return slice(g * HEAD_DIM, (g + 1) * HEAD_DIM)

    def step(kb, n_kb, mask):
        width = n_kb * blk
        off = pl.multiple_of(kb * blk, blk)
        kr = kr_ref[pl.ds(off, width), :]
        ones = jnp.ones((ONES_ROWS, width), BF16)
        ss = []
        for g in range(heads):
            q = jnp.concatenate([qn_ref[:, lanes(g)], qr_ref[:, lanes(g)]], axis=1)
            k = jnp.concatenate([kn_ref[pl.ds(off, width), lanes(g)], kr], axis=1)
            s = lax.dot_general(k, q, (((1,), (1,)), ((), ())), preferred_element_type=F32)
            ss.append(s if mask is None else jnp.where(mask, s, NEG_BIG))
        ps, alphas = [], []
        for g, s in enumerate(ss):
            m_old = m_ref[g][0:1, :]
            m_new = jnp.maximum(m_old, jnp.max(s, axis=0, keepdims=True))
            ps.append(jnp.exp2(s - m_new).astype(BF16))
            alphas.append(jnp.exp2(m_old - m_new))
            m_ref[g] = jnp.broadcast_to(m_new, m_ref.shape[1:])
        for g in range(heads):
            vt = jnp.concatenate([vt_ref[kb + n, lanes(g), :] for n in range(n_kb)] , axis=1)
            vt = jnp.concatenate([vt, ones], axis=0)
            acc_ref[g] = acc_ref[g] * alphas[g] + jnp.dot(vt, ps[g], preferred_element_type=F32)

    def body(j, _):
        step(2 * j, 2, None)
        return 0

    lax.fori_loop(0, qi // 2, body, 0)

    @pl.when(qi % 2 == 1)
    def _():
        step(qi - 1, 1, None)

    key_chunk = lax.broadcasted_iota(jnp.int32, (blk, blk), 0) // CHUNK
    query_chunk = lax.broadcasted_iota(jnp.int32, (blk, blk), 1) // CHUNK
    step(qi, 1, key_chunk <= query_chunk)
    for g in range(heads):
        acc = acc_ref[g]
        out_t = acc[:HEAD_DIM, :] / acc[HEAD_DIM:HEAD_DIM + 1, :]
        o_ref[:, lanes(g)] = out_t.T.astype(o_ref.dtype)


def _mla_attention(qn, qr, kn, kr, vt, B, S, w_casts, layer):
    blk = _tile(S, MLA_BLOCK)
    G = MLA_HEAD_GROUP
    gw = G * HEAD_DIM
    grid = (B, HEADS // G, S // blk)
    wf_specs, wb_specs, wb_shapes = zip(*[_slab_specs(w, layer, grid) for w in w_casts])
    sh3 = lambda a: a.reshape(B, S, a.shape[-1])
    once = pl.Buffered(1)
    q_spec = pl.BlockSpec((None, blk, gw), lambda b, h, i: (b, i, h))
    k_spec = pl.BlockSpec((None, S, gw), lambda b, h, i: (b, 0, h), pipeline_mode=once)
    kr_spec = pl.BlockSpec((None, S, LANE), lambda b, h, i: (b, 0, 0), pipeline_mode=once)
    vt_spec = pl.BlockSpec((None, S // blk, gw, blk), lambda b, h, i: (b, 0, h, 0), pipeline_mode=once)
    outs = pl.pallas_call(
        functools.partial(_mla_kernel, blk=blk, heads=G, n_cast=len(w_casts)),
        grid=grid,
        in_specs=[q_spec, q_spec, k_spec, kr_spec, vt_spec, *wf_specs],
        out_specs=[q_spec, *wb_specs],
        out_shape=[jax.ShapeDtypeStruct((B, S, ATT_WIDTH), BF16), *wb_shapes],
        scratch_shapes=[pltpu.VMEM((G, HEAD_DIM + ONES_ROWS, blk), F32), pltpu.VMEM((G, 8, blk), F32)],
        compiler_params=_params(("parallel", "parallel", "arbitrary")),
        name="latent_attention",
    )(sh3(qn), sh3(qr), sh3(kn), sh3(kr), vt, *w_casts)
    return outs[0], outs[1:]


def _sigmoid(x):
    return 1.0 / (1.0 + jnp.exp(-x))


def _merge_kernel(osb_ref, omla_ref, wsb_ref, wmla_ref, gsb_ref, gmla_ref, o_ref):
    a = jnp.dot(osb_ref[...], wsb_ref[...], preferred_element_type=F32)
    b = jnp.dot(omla_ref[...], wmla_ref[...], preferred_element_type=F32)
    o_ref[...] = (_sigmoid(gsb_ref[...].astype(F32)) * a
                  + _sigmoid(gmla_ref[...].astype(F32)) * b).astype(o_ref.dtype)


def _merge(o_sb, o_mla, w_sb, w_mla, proj):
    T, W = o_sb.shape
    D = w_sb.shape[1]
    tm, tn = _tile(T, 1024), _tile(D, 1024)
    gate0 = COL_GATE // tn
    act = pl.BlockSpec((tm, W), lambda i, j: (i, 0))
    wsp = pl.BlockSpec((W, tn), lambda i, j: (0, j))
    return pl.pallas_call(
        _merge_kernel,
        grid=(T // tm, D // tn),
        in_specs=[act, act, wsp, wsp,
                  pl.BlockSpec((tm, tn), lambda i, j: (i, gate0 + j)),
                  pl.BlockSpec((tm, tn), lambda i, j: (i, gate0 + D // tn + j))],
        out_specs=pl.BlockSpec((tm, tn), lambda i, j: (i, j)),
        out_shape=jax.ShapeDtypeStruct((T, D), BF16),
        compiler_params=_params(("parallel", "arbitrary")),
        name="gated_branch_merge",
    )(o_sb, o_mla, w_sb, w_mla, proj, proj)


def _mlp_kernel(h_ref, wu_ref, wd_ref, o_ref, acc_ref):
    f = pl.program_id(1)

    @pl.when(f == 0)
    def _():
        acc_ref[...] = jnp.zeros_like(acc_ref)

    u = jnp.dot(h_ref[...], wu_ref[...], preferred_element_type=F32)
    a = jnp.square(jnp.maximum(u, 0.0)).astype(BF16)
    acc_ref[...] += jnp.dot(a, wd_ref[...], preferred_element_type=F32)

    @pl.when(f == pl.num_programs(1) - 1)
    def _():
        o_ref[...] = acc_ref[...].astype(o_ref.dtype)


def _mlp(h, w_up, w_down):
    T, D = h.shape
    F = w_up.shape[1]
    tm, tf = _tile(T, 512), _tile(F, 512)
    return pl.pallas_call(
        _mlp_kernel,
        grid=(T // tm, F // tf),
        in_specs=[pl.BlockSpec((tm, D), lambda i, f: (i, 0)),
                  pl.BlockSpec((D, tf), lambda i, f: (0, f)),
                  pl.BlockSpec((tf, D), lambda i, f: (f, 0))],
        out_specs=pl.BlockSpec((tm, D), lambda i, f: (i, 0)),
        out_shape=jax.ShapeDtypeStruct((T, D), BF16),
        scratch_shapes=[pltpu.VMEM((tm, D), F32)],
        compiler_params=_params(("parallel", "arbitrary")),
        name="relu2_mlp",
    )(h, w_up, w_down)


KVD_END = COL_KVD + KV_RANK + ROPE_DIM
GATE_SHIFT = COL_GATE - KVD_END


def _w_in_block(w_ref, o_ref, row0):
    r = row0 + lax.broadcasted_iota(jnp.int32, o_ref.shape, 0)
    w = w_ref[...]
    w = jnp.where(r < ATT_WIDTH, w * (HEAD_DIM ** -0.5), w)
    keep = jnp.where(r < KVD_END, 1, 0) + jnp.where(r >= COL_GATE, 1, 0)
    o_ref[...] = jnp.where(keep > 0, w, 0.0).astype(BF16)


def _w_in_window(wt, n_blocks, where):
    _, n_src, D = wt.shape
    n_pad = n_src + GATE_SHIFT
    rows = n_pad // n_blocks
    assert rows * n_blocks == n_pad and rows % 16 == 0
    unit = 8
    while rows % (2 * unit) == 0 and GATE_SHIFT % (2 * unit) == 0:
        unit *= 2
    assert all(not (j * rows < KVD_END and (j + 1) * rows > COL_GATE) for j in range(n_blocks))

    def src(*ids):
        layer, j = where(*ids)
        shift = jnp.where(j * rows < KVD_END, 0, GATE_SHIFT // unit)
        return layer, (j * (rows // unit) - shift) * unit, 0

    return pl.BlockSpec((None, pl.Element(rows), pl.Element(D)), src), rows, n_pad


def _w_in_kernel(w_ref, o_ref):
    _w_in_block(w_ref, o_ref, pl.program_id(0) * o_ref.shape[0])


def _prep_w_in(wt, layer):
    n_blocks = (wt.shape[1] + GATE_SHIFT) // 512
    src_spec, rows, n_pad = _w_in_window(wt, n_blocks, lambda j: (layer, j))
    return pl.pallas_call(
        _w_in_kernel,
        grid=(n_blocks,),
        in_specs=[src_spec],
        out_specs=pl.BlockSpec((rows, wt.shape[2]), lambda j: (j, 0)),
        out_shape=jax.ShapeDtypeStruct((n_pad, wt.shape[2]), BF16),
        compiler_params=_params(("parallel",)),
        name="input_weight_layout",
    )(wt)


def _prep_w_uq(w):
    w = w.reshape(Q_RANK, HEADS, HEAD_DIM + ROPE_DIM)
    nope = w[:, :, :HEAD_DIM].reshape(Q_RANK, ATT_WIDTH)
    rope = jnp.pad(w[:, :, HEAD_DIM:], ((0, 0), (0, 0), (0, LANE - ROPE_DIM))).reshape(Q_RANK, HEADS * LANE)
    return jnp.concatenate([nope, rope], axis=1).astype(BF16)


def _prep_w_ukv(w):
    w = w.reshape(KV_RANK, HEADS, 2 * HEAD_DIM)
    k = w[:, :, :HEAD_DIM].reshape(KV_RANK, ATT_WIDTH)
    v = w[:, :, HEAD_DIM:].reshape(KV_RANK, ATT_WIDTH)
    return jnp.concatenate([k, v], axis=1).astype(BF16)


def kernel(x, c, positions, w_ada, b_ada, g_pre_mix, g_post_mix, g_pre_mlp, g_post_mlp, w_in,
           g_q_lora, w_uq, g_kv_lora, w_ukv, w_o_sb, w_o_mla, w_out, w_up, w_down):
    B, S, D = x.shape
    L = w_ada.shape[0]
    T = B * S
    assert S % CHUNK == 0 and D % 512 == 0

    rows = 16
    c_pad = jnp.zeros((rows, D), F32).at[:B].set(c)
    ride_ada = _ada_can_ride(T, D)
    six = lambda a: a[:B].reshape(B, 6, 1, D)
    ada = [six(_ada(c_pad, w_ada, b_ada, l)) if (l == 0 or not ride_ada) else None for l in range(L)]
    cos_t, sin_t = _rope_tables(positions)

    w_in_t = jnp.swapaxes(w_in, 1, 2)
    w_in_b = _prep_w_in(w_in_t, 0)

    x2d = x.reshape(T, D)
    h = _prenorm(x2d, g_pre_mix[0], ada[0], 0, 1, S)
    for l in range(L):
        more = l + 1 < L
        proj, ada_next = _matmul_nt(h, w_in_b, BF16, "input_projection",
                                    ada=(c_pad, w_ada, b_ada, l + 1) if more and ride_ada else None)
        if ada_next is not None:
            ada[l + 1] = six(ada_next)
        o_sb, sb_extra = _sb_attention(proj.reshape(B, S, -1), (w_o_sb, w_o_mla, w_out), l,
                                       w_in_next=(w_in_t, l + 1) if more else None)
        w_o_sb_b, w_o_mla_b, w_out_b = sb_extra[:3]
        if more:
            w_in_b = sb_extra[3]
        o_sb = o_sb.reshape(T, ATT_WIDTH)
        qn, qr, kn, vt, kr = _mla_prep(proj, g_q_lora[l], g_kv_lora[l], _prep_w_uq(w_uq[l]),
                                       _prep_w_ukv(w_ukv[l]), cos_t, sin_t, B, S)
        o_mla, (w_up_b, w_down_b) = _mla_attention(qn, qr, kn, kr, vt, B, S, (w_up, w_down), l)
        o_mla = o_mla.reshape(T, ATT_WIDTH)
        merged = _merge(o_sb, o_mla, w_o_sb_b, w_o_mla_b, proj)
        y = _matmul(merged, w_out_b, BF16, "output_projection")
        x2d, h = _post(y, x2d, g_post_mix[l], ada[l], 2, S, nxt=(g_pre_mlp[l], ada[l], 3, 4))
        y = _mlp(h, w_up_b, w_down_b)
        nxt = (g_pre_mix[l + 1], ada[l + 1], 0, 1) if more else None
        x2d, h = _post(y, x2d, g_post_mlp[l], ada[l], 5, S, nxt=nxt)
    return x2d.reshape(B, S, D)
```
